```python
import jax, jax.numpy as jnp
from jax import lax
import numpy as np

D_MODEL = 1024
BATCH = 32
SEQ = 2048
DEPTH = 1
DEC_BATCH = 8
DEC_SEQ = 16
PAST_LEN = 4096

CHUNK = 64
N_CONV_HEADS = 8
CONV_HEAD_DIM = D_MODEL // 16
CONV_CH = N_CONV_HEADS * CONV_HEAD_DIM
CONV_WIDTH = 3
POOL_WINDOWS = (2, 4, 8, 16)
N_POOL_GROUPS = len(POOL_WINDOWS)
POOL_GROUP_CH = D_MODEL // 8
POOL_CH = N_POOL_GROUPS * POOL_GROUP_CH
POOL_STATE = max(POOL_WINDOWS) - 1
D_MIX = CONV_CH + POOL_CH
D_IN = 3 * CONV_CH + POOL_CH
N_EXPERTS = 32
TOP_K = 4
D_EXPERT = D_MODEL
SWIGLU_LIMIT = 7.0
SWIGLU_ALPHA = 1.702
MOE_BLOCK = 256
D_PLE = 256
RMS_EPS = 1e-6

kernel_name = "hybrid_conv_pool_moe_stream_step"


def rmsnorm(x, g):
    x32 = x.astype(jnp.float32)
    y = x32 * lax.rsqrt(jnp.mean(x32 * x32, axis=-1, keepdims=True) + RMS_EPS)
    return (y * g.astype(jnp.float32)).astype(x.dtype)


def short_conv_mixer(bg, cg, hv, conv_prev, w_conv):
    v = cg * hv
    ext = jnp.concatenate([conv_prev.astype(v.dtype), v], axis=1)
    S = v.shape[1]
    out = ext[:, 0:S] * w_conv[0]
    for k in range(1, CONV_WIDTH):
        out = out + ext[:, k:k + S] * w_conv[k]
    return bg * out, ext[:, -(CONV_WIDTH - 1):]


def pool_mixer(u, pool_prev, start_pos, w_pool, pool_scale):
    B, S, _ = u.shape
    ext = jnp.concatenate([pool_prev.astype(u.dtype), u], axis=1)
    cs = jnp.cumsum(ext.astype(jnp.float32), axis=1)
    cs = jnp.concatenate([jnp.zeros_like(cs[:, :1]), cs], axis=1)
    pos = start_pos + jnp.arange(S)
    end = POOL_STATE + 1
    means = []
    for g, w in enumerate(POOL_WINDOWS):
        sl = slice(g * POOL_GROUP_CH, (g + 1) * POOL_GROUP_CH)
        win = cs[:, end:end + S, sl] - cs[:, end - w:end - w + S, sl]
        cnt = jnp.minimum(pos + 1, w).astype(jnp.float32)[None, :, None]
        means.append(win / cnt)
    diff = jnp.concatenate(means, axis=-1) - u.astype(jnp.float32)
    diff = diff.reshape(B, S, N_POOL_GROUPS, POOL_GROUP_CH)
    y = jnp.einsum('bsgc,gcd->bsgd', diff, w_pool.astype(jnp.float32)).reshape(B, S, POOL_CH)
    return (y * pool_scale.astype(jnp.float32)).astype(u.dtype), ext[:, -POOL_STATE:]


def moe(x, w_router, b_router, w_gu, b_gu, w_down, b_down):
    shp = x.shape
    xt = x.reshape(-1, D_MODEL)
    T = xt.shape[0]
    logits = xt.astype(jnp.float32) @ w_router.astype(jnp.float32) + b_router.astype(jnp.float32)
    top_logit, top_e = lax.top_k(logits, TOP_K)
    gates = jax.nn.softmax(top_logit, axis=-1)
    N = T * TOP_K
    flat_e = top_e.reshape(-1).astype(jnp.int32)
    flat_tok = jnp.repeat(jnp.arange(T, dtype=jnp.int32), TOP_K)
    flat_g = gates.reshape(-1)
    order = jnp.argsort(flat_e)
    se, stok, sg = flat_e[order], flat_tok[order], flat_g[order]
    counts = jnp.bincount(flat_e, length=N_EXPERTS).astype(jnp.int32)
    starts = jnp.cumsum(counts) - counts
    padded = ((counts + MOE_BLOCK - 1) // MOE_BLOCK) * MOE_BLOCK
    pend = jnp.cumsum(padded)
    pstarts = pend - padded
    dest = pstarts[se] + jnp.arange(N, dtype=jnp.int32) - starts[se]
    n_blocks = -(-N // MOE_BLOCK) + N_EXPERTS
    P = n_blocks * MOE_BLOCK
    slot_tok = jnp.full((P,), T, jnp.int32).at[dest].set(stok)
    slot_gate = jnp.zeros((P,), jnp.float32).at[dest].set(sg)
    block_e = jnp.minimum(
        jnp.searchsorted(pend, jnp.arange(n_blocks, dtype=jnp.int32) * MOE_BLOCK, side='right'),
        N_EXPERTS - 1).astype(jnp.int32)
    x_pad = jnp.concatenate([xt, jnp.zeros((1, D_MODEL), xt.dtype)], axis=0)

    def expert_block(args):
        tok, e = args
        xb = x_pad[tok]
        gu = xb @ w_gu[e] + b_gu[e]
        glu = jnp.minimum(gu[:, :D_EXPERT], SWIGLU_LIMIT)
        lin = jnp.clip(gu[:, D_EXPERT:], -SWIGLU_LIMIT, SWIGLU_LIMIT)
        act = (lin + 1.0) * glu * jax.nn.sigmoid(SWIGLU_ALPHA * glu)
        return act @ w_down[e] + b_down[e]

    ys = lax.map(expert_block, (slot_tok.reshape(n_blocks, MOE_BLOCK), block_e))
    ys = ys.reshape(P, D_MODEL).astype(jnp.float32) * slot_gate[:, None]
    out = jnp.zeros((T + 1, D_MODEL), jnp.float32).at[slot_tok].add(ys)
    return out[:T].astype(x.dtype).reshape(shp)


def encoder_layer(h, p, conv_prev, pool_prev, start_pos,
                  g_mix, w_in, w_conv, w_pool, pool_scale, w_out,
                  g_ffn, w_router, b_router, w_gu, b_gu, w_down, b_down,
                  g_ple, w_ple_proj, w_ple_gate):
    z = rmsnorm(h, g_mix) @ w_in
    bg, cg, hv, u = jnp.split(z, [CONV_CH, 2 * CONV_CH, 3 * CONV_CH], axis=-1)
    y_a, conv_new = short_conv_mixer(bg, cg, hv, conv_prev, w_conv)
    y_b, pool_new = pool_mixer(u, pool_prev, start_pos, w_pool, pool_scale)
    h = h + jnp.concatenate([y_a, y_b], axis=-1) @ w_out
    h = h + moe(rmsnorm(h, g_ffn), w_router, b_router, w_gu, b_gu, w_down, b_down)
    gate = jax.nn.sigmoid((rmsnorm(h, g_ple) @ w_ple_gate).astype(jnp.float32))
    h = h + ((p @ w_ple_proj).astype(jnp.float32) * gate).astype(h.dtype)
    return h, conv_new, pool_new


def setup_inputs(seed: int = 0) -> dict:
    key = jax.random.key(seed)
    ks = jax.random.split(key, 24)
    f32 = jnp.float32
    nrm = lambda k, shape, s: jax.random.normal(k, shape, f32) * s
    return {
        "x_prompt": nrm(ks[0], (BATCH, SEQ, D_MODEL), 1.0),
        "x_sample": nrm(ks[1], (DEC_BATCH, DEC_SEQ, D_MODEL), 1.0),
        "p_prompt": nrm(ks[2], (DEPTH, BATCH, SEQ, D_PLE), 1.0),
        "p_sample": nrm(ks[3], (DEPTH, DEC_BATCH, DEC_SEQ, D_PLE), 1.0),
        "state_conv": nrm(ks[4], (DEPTH, DEC_BATCH, CONV_WIDTH - 1, CONV_CH), 1.0),
        "state_pool": nrm(ks[5], (DEPTH, DEC_BATCH, POOL_STATE, POOL_CH), 1.0),
        "g_mix": 1.0 + nrm(ks[6], (DEPTH, D_MODEL), 0.02),
        "w_in": nrm(ks[7], (DEPTH, D_MODEL, D_IN), D_MODEL ** -0.5),
        "w_conv": nrm(ks[8], (DEPTH, CONV_WIDTH, CONV_CH), CONV_WIDTH ** -0.5),
        "w_pool": nrm(ks[9], (DEPTH, N_POOL_GROUPS, POOL_GROUP_CH, POOL_GROUP_CH), POOL_GROUP_CH ** -0.5),
        "pool_scale": 1.0 + nrm(ks[10], (DEPTH, POOL_CH), 0.1),
        "w_out": nrm(ks[11], (DEPTH, D_MIX, D_MODEL), D_MIX ** -0.5),
        "g_ffn": 1.0 + nrm(ks[12], (DEPTH, D_MODEL), 0.02),
        "w_router": nrm(ks[13], (DEPTH, D_MODEL, N_EXPERTS), D_MODEL ** -0.5),
        "b_router": nrm(ks[14], (DEPTH, N_EXPERTS), 0.01),
        "w_gu": nrm(ks[15], (DEPTH, N_EXPERTS, D_MODEL, 2 * D_EXPERT), D_MODEL ** -0.5),
        "b_gu": nrm(ks[16], (DEPTH, N_EXPERTS, 2 * D_EXPERT), 0.01),
        "w_down": nrm(ks[17], (DEPTH, N_EXPERTS, D_EXPERT, D_MODEL), D_EXPERT ** -0.5),
        "b_down": nrm(ks[18], (DEPTH, N_EXPERTS, D_MODEL), 0.01),
        "g_ple": 1.0 + nrm(ks[19], (DEPTH, D_MODEL), 0.02),
        "w_ple_proj": nrm(ks[20], (DEPTH, D_PLE, D_MODEL), D_PLE ** -0.5),
        "w_ple_gate": nrm(ks[21], (DEPTH, D_MODEL, D_MODEL), D_MODEL ** -0.5),
        "g_final": 1.0 + nrm(ks[22], (D_MODEL,), 0.02),
    }


def reference(x_prompt, x_sample, p_prompt, p_sample, state_conv, state_pool,
              g_mix, w_in, w_conv, w_pool, pool_scale, w_out,
              g_ffn, w_router, b_router, w_gu, b_gu, w_down, b_down,
              g_ple, w_ple_proj, w_ple_gate, g_final):
    hp, hs = x_prompt, x_sample
    bp = x_prompt.shape[0]
    conv_p, pool_p, conv_s, pool_s = [], [], [], []
    for l in range(DEPTH):
        wl = (g_mix[l], w_in[l], w_conv[l], w_pool[l], pool_scale[l], w_out[l],
              g_ffn[l], w_router[l], b_router[l], w_gu[l], b_gu[l], w_down[l], b_down[l],
              g_ple[l], w_ple_proj[l], w_ple_gate[l])
        zero_conv = jnp.zeros((bp, CONV_WIDTH - 1, CONV_CH), hp.dtype)
        zero_pool = jnp.zeros((bp, POOL_STATE, POOL_CH), hp.dtype)
        hp, cpn, ppn = encoder_layer(hp, p_prompt[l], zero_conv, zero_pool, 0, *wl)
        hs, csn, psn = encoder_layer(hs, p_sample[l], state_conv[l], state_pool[l], PAST_LEN, *wl)
        conv_p.append(cpn)
        pool_p.append(ppn)
        conv_s.append(csn)
        pool_s.append(psn)
    y_prompt = rmsnorm(hp, g_final)
    y_sample = rmsnorm(hs, g_final)
    return (y_prompt, y_sample, jnp.stack(conv_p), jnp.stack(pool_p), jnp.stack(conv_s), jnp.stack(pool_s))
```

```python
import functools

import jax
import jax.numpy as jnp
from jax import lax
from jax.experimental import pallas as pl
from jax.experimental.pallas import tpu as pltpu

F32 = jnp.float32
BF16 = jnp.bfloat16
I32 = jnp.int32

PAST_LEN = 4096
POOL_WINDOWS = (2, 4, 8, 16)
TOP_K = 4
SWIGLU_LIMIT = 7.0
SWIGLU_ALPHA = 1.702
RMS_EPS = 1e-6

CONV_HIST = 8
POOL_HIST = 16
LANES = 128
VMEM_LIMIT = 56 * 1024 * 1024

MIX_ROWS = 512
EXPERT_ROWS = 512
COMBINE_ROWS = 256
DISPATCH_ROWS = 512


def _rms(x, g):
    return x * lax.rsqrt(jnp.mean(x * x, axis=-1, keepdims=True) + RMS_EPS) * g


def _dot(a, b):
    return jnp.dot(a, b, preferred_element_type=F32)


def _mixer_kernel(start_pos, nb, ts, n_exp,
                  x_ref, convp_ref, poolp_ref, gmix_ref, win_ref, wconv_ref, wpool_ref,
                  pscale_ref, wout_ref, gffn_ref, wr_ref, br_ref, tri_ref,
                  h1_ref, xf_ref, tope_ref, gate_ref, rank_ref, cnt_ref, convn_ref, pooln_ref,
                  ev_ref, eu_ref, carry_ref):
    b = pl.program_id(0)
    s = pl.program_id(1)
    rows = nb * ts
    d_model = x_ref.shape[-1]
    cch = wconv_ref.shape[-1]
    gch = wpool_ref.shape[-1]
    n_grp = wpool_ref.shape[0]

    @pl.when((b == 0) & (s == 0))
    def _():
        carry_ref[...] = jnp.zeros_like(carry_ref)

    @pl.when(s == 0)
    def _():
        ev_ref[:, 0:CONV_HIST, :] = convp_ref[...]
        eu_ref[:, 0:POOL_HIST, :] = poolp_ref[...]

    x = x_ref[...].reshape(rows, d_model)
    xn = _rms(x, gmix_ref[...])
    z = _dot(xn.astype(BF16), win_ref[...])
    bg = z[:, 0:cch]
    v = z[:, cch:2 * cch] * z[:, 2 * cch:3 * cch]
    u = z[:, 3 * cch:]

    w0 = wconv_ref[0:1, :]
    w1 = wconv_ref[1:2, :]
    w2 = wconv_ref[2:3, :]
    pos = start_pos + s * ts + lax.broadcasted_iota(I32, (ts, 1), 0)

    mix_rows = []
    for i in range(nb):
        r0 = i * ts
        v_i = v[r0:r0 + ts]
        u_i = u[r0:r0 + ts]
        ev_ref[i, CONV_HIST:CONV_HIST + ts, :] = v_i
        eu_ref[i, POOL_HIST:POOL_HIST + ts, :] = u_i
        conv = (w0 * ev_ref[i, CONV_HIST - 2:CONV_HIST - 2 + ts, :]
                + w1 * ev_ref[i, CONV_HIST - 1:CONV_HIST - 1 + ts, :]
                + w2 * v_i)
        y_a = bg[r0:r0 + ts] * conv

        e = eu_ref[i]
        sums = []
        cur = e
        shift = 1
        for g in range(n_grp):
            cur = cur + pltpu.roll(cur, shift, axis=0)
            sums.append(cur[POOL_HIST:POOL_HIST + ts, 0:gch])
            if g + 1 < n_grp:
                cur = cur[:, gch:]
            shift *= 2
        yb = []
        for g, w in enumerate(POOL_WINDOWS):
            cnt = jnp.minimum(pos + 1, w).astype(F32)
            diff = sums[g] / cnt - u_i[:, g * gch:(g + 1) * gch]
            yb.append(_dot(diff.astype(BF16), wpool_ref[g]))
        y_b = jnp.concatenate(yb, axis=1) * pscale_ref[...]
        mix_rows.append(jnp.concatenate([y_a, y_b], axis=1))

        convn_ref[i] = ev_ref[i, ts + CONV_HIST - 2:ts + CONV_HIST, :]
        pooln_ref[i] = eu_ref[i, ts + 1:ts + POOL_HIST, :]
        ev_ref[i, 0:CONV_HIST, :] = ev_ref[i, ts:ts + CONV_HIST, :]
        eu_ref[i, 0:POOL_HIST, :] = eu_ref[i, ts:ts + POOL_HIST, :]

    mix = mix_rows[0] if nb == 1 else jnp.concatenate(mix_rows, axis=0)
    h1 = x + _dot(mix.astype(BF16), wout_ref[...])
    h1_ref[...] = h1.reshape(nb, ts, d_model)
    xf = _rms(h1, gffn_ref[...])
    xf_ref[...] = xf.reshape(nb, ts, d_model)

    lg = _dot(xf.astype(BF16), wr_ref[...])
    lgt = lg.T
    logits = lgt[0:n_exp] + lgt[n_exp:2 * n_exp] + br_ref[...]

    iota_e = lax.broadcasted_iota(I32, logits.shape, 0)
    top_l, top_e = [], []
    sel = jnp.zeros(logits.shape, F32)
    cur = logits
    for _ in range(TOP_K):
        m = jnp.max(cur, axis=0, keepdims=True)
        idx = jnp.min(jnp.where(cur == m, iota_e, n_exp), axis=0, keepdims=True)
        hit = iota_e == idx
        top_l.append(m)
        top_e.append(idx)
        sel = jnp.where(hit, 1.0, sel)
        cur = jnp.where(hit, -jnp.inf, cur)
    ex = [jnp.exp(l - top_l[0]) for l in top_l]
    den = ex[0] + ex[1] + ex[2] + ex[3]
    gate_ref[...] = jnp.concatenate([e_ / den for e_ in ex], axis=0)
    tope_ref[...] = jnp.concatenate(top_e, axis=0)

    excl = _dot(sel.astype(BF16), tri_ref[...]) + carry_ref[:, 0:1]
    ranks = [jnp.sum(jnp.where(iota_e == e_, excl, 0.0), axis=0, keepdims=True) for e_ in top_e]
    rank_ref[...] = jnp.concatenate(ranks, axis=0).astype(I32)
    carry = carry_ref[...] + jnp.sum(sel, axis=1, keepdims=True)
    carry_ref[...] = carry
    cnt_ref[...] = carry.astype(I32)


def _mixer(x, conv_prev, pool_prev, start_pos, w, nb, ts):
    bsz, seq, d_model = x.shape
    n_s = seq // ts
    n_b = bsz // nb
    rows = nb * ts
    tok = bsz * seq
    cch = w["w_conv"].shape[-1]
    pch = w["pool_scale"].shape[-1]
    n_exp = w["b_router"].shape[0]
    tri = (lax.broadcasted_iota(I32, (rows, rows), 0)
           < lax.broadcasted_iota(I32, (rows, rows), 1)).astype(BF16)

    def full(a):
        return pl.BlockSpec(a.shape, lambda b, s: (0,) * a.ndim)

    weights = [w["g_mix"], w["w_in"], w["w_conv"], w["w_pool"], w["pool_scale"], w["w_out"],
               w["g_ffn"], w["w_router"], w["b_router"], tri]
    tok_spec = pl.BlockSpec((TOP_K, rows), lambda b, s: (0, b * n_s + s))
    return pl.pallas_call(
        functools.partial(_mixer_kernel, start_pos, nb, ts, n_exp),
        grid=(n_b, n_s),
        in_specs=[pl.BlockSpec((nb, ts, d_model), lambda b, s: (b, s, 0)),
                  pl.BlockSpec((nb, CONV_HIST, cch), lambda b, s: (b, 0, 0)),
                  pl.BlockSpec((nb, POOL_HIST, pch), lambda b, s: (b, 0, 0))]
                 + [full(a) for a in weights],
        out_specs=[pl.BlockSpec((nb, ts, d_model), lambda b, s: (b, s, 0)),
                   pl.BlockSpec((nb, ts, d_model), lambda b, s: (b, s, 0)),
                   tok_spec, tok_spec, tok_spec,
                   pl.BlockSpec((n_exp, LANES), lambda b, s: (0, 0)),
                   pl.BlockSpec((nb, 2, cch), lambda b, s: (b, 0, 0)),
                   pl.BlockSpec((nb, POOL_HIST - 1, pch), lambda b, s: (b, 0, 0))],
        out_shape=[jax.ShapeDtypeStruct((bsz, seq, d_model), F32),
                   jax.ShapeDtypeStruct((bsz, seq, d_model), F32),
                   jax.ShapeDtypeStruct((TOP_K, tok), I32),
                   jax.ShapeDtypeStruct((TOP_K, tok), F32),
                   jax.ShapeDtypeStruct((TOP_K, tok), I32),
                   jax.ShapeDtypeStruct((n_exp, LANES), I32),
                   jax.ShapeDtypeStruct((bsz, 2, cch), F32),
                   jax.ShapeDtypeStruct((bsz, POOL_HIST - 1, pch), F32)],
        scratch_shapes=[pltpu.VMEM((nb, ts + CONV_HIST, cch), F32),
                        pltpu.VMEM((nb, ts + POOL_HIST, pch), F32),
                        pltpu.VMEM((n_exp, LANES), F32)],
        compiler_params=pltpu.CompilerParams(
            dimension_semantics=("arbitrary", "arbitrary"), vmem_limit_bytes=VMEM_LIMIT),
        name="mixer",
    )(x, conv_prev, pool_prev, *weights)


def _dispatch_kernel(tt, tsamp, n_p, tm, n_blk, n_exp,
                     zb_ref, nu_ref, dp_ref, ds_ref, xfp_ref, xfs_ref, xs_ref,
                     zero_ref, sem, zsem):
    i = pl.program_id(0)

    def zero_block(j):
        return pltpu.make_async_copy(zero_ref, xs_ref.at[pl.ds(j * tm, tm)], zsem)

    @pl.when(i == 0)
    def _():
        zero_ref[...] = jnp.zeros_like(zero_ref)

        def each_block(fn):
            def tail(e, c):
                @pl.when(zb_ref[e] >= 0)
                def _():
                    fn(zero_block(zb_ref[e]))
                return c
            lax.fori_loop(0, n_exp, tail, 0)

            def unused(j, c):
                fn(zero_block(j))
                return c
            lax.fori_loop(nu_ref[0], n_blk, unused, 0)

        each_block(lambda cp: cp.start())
        each_block(lambda cp: cp.wait())

    def scatter(d_ref, x_ref, n):
        def body(t, c):
            for k in range(TOP_K):
                d = d_ref[k * n + t]
                pltpu.make_async_copy(x_ref.at[pl.ds(t, 1)], xs_ref.at[pl.ds(d, 1)], sem).start()
            return c
        lax.fori_loop(0, n, body, 0)
        pltpu.make_async_copy(xs_ref.at[pl.ds(0, TOP_K * n)], xs_ref.at[pl.ds(0, TOP_K * n)],
                              sem).wait()

    @pl.when(i < n_p)
    def _():
        scatter(dp_ref, xfp_ref, tt)

    @pl.when(i == n_p)
    def _():
        scatter(ds_ref, xfs_ref, tsamp)


def _dispatch(zblk, n_used, dest_p, dest_s, xf_p, xf_s, n_blk, tm, tt):
    tp, d_model = xf_p.shape
    tsamp = xf_s.shape[0]
    n_p = tp // tt
    n_exp = zblk.shape[0]

    def prompt_map(i, zb, nu):
        return (jnp.minimum(i, n_p - 1),)

    return pl.pallas_call(
        functools.partial(_dispatch_kernel, tt, tsamp, n_p, tm, n_blk, n_exp),
        grid_spec=pltpu.PrefetchScalarGridSpec(
            num_scalar_prefetch=2, grid=(n_p + 1,),
            in_specs=[pl.BlockSpec((TOP_K * tt,), prompt_map, memory_space=pltpu.SMEM),
                      pl.BlockSpec((TOP_K * tsamp,), lambda i, zb, nu: (0,),
                                   memory_space=pltpu.SMEM),
                      pl.BlockSpec((tt, d_model), lambda i, zb, nu: prompt_map(i, zb, nu) + (0,)),
                      pl.BlockSpec((tsamp, d_model), lambda i, zb, nu: (0, 0))],
            out_specs=pl.BlockSpec(memory_space=pl.ANY),
            scratch_shapes=[pltpu.VMEM((tm, d_model), F32),
                            pltpu.SemaphoreType.DMA, pltpu.SemaphoreType.DMA]),
        out_shape=jax.ShapeDtypeStruct((n_blk * tm, d_model), F32),
        compiler_params=pltpu.CompilerParams(dimension_semantics=("arbitrary",)),
        name="dispatch",
    )(zblk, n_used, dest_p, dest_s, xf_p, xf_s)


def _expert_kernel(be_ref, nu_ref, x_ref, wgu_ref, bgu_ref, wd_ref, bd_ref, y_ref):
    del be_ref
    d_exp = wd_ref.shape[1]

    used = pl.program_id(0) < nu_ref[0]

    @pl.when(jnp.logical_not(used))
    def _():
        y_ref[...] = jnp.zeros_like(y_ref)

    @pl.when(used)
    def _():
        gu = _dot(x_ref[...].astype(BF16), wgu_ref[0]) + bgu_ref[0]
        glu = jnp.minimum(gu[:, :d_exp], SWIGLU_LIMIT)
        lin = jnp.clip(gu[:, d_exp:], -SWIGLU_LIMIT, SWIGLU_LIMIT)
        act = (lin + 1.0) * glu * jax.nn.sigmoid(SWIGLU_ALPHA * glu)
        y_ref[...] = _dot(act.astype(BF16), wd_ref[0]) + bd_ref[0]


def _experts(block_e, n_used, xs, w_gu, b_gu, w_down, b_down, tm):
    n_rows, d_model = xs.shape
    n_exp, _, d_gu = w_gu.shape
    d_exp = w_down.shape[1]

    def row_map(i, be, nu):
        return (jnp.minimum(i, nu[0] - 1), 0)

    def w_map(i, be, nu):
        return (be[i], 0, 0)

    return pl.pallas_call(
        _expert_kernel,
        grid_spec=pltpu.PrefetchScalarGridSpec(
            num_scalar_prefetch=2, grid=(n_rows // tm,),
            in_specs=[pl.BlockSpec((tm, d_model), row_map),
                      pl.BlockSpec((1, d_model, d_gu), w_map),
                      pl.BlockSpec((1, 1, d_gu), w_map),
                      pl.BlockSpec((1, d_exp, d_model), w_map),
                      pl.BlockSpec((1, 1, d_model), w_map)],
            out_specs=pl.BlockSpec((tm, d_model), lambda i, be, nu: (i, 0))),
        out_shape=jax.ShapeDtypeStruct((n_rows, d_model), F32),
        compiler_params=pltpu.CompilerParams(
            dimension_semantics=("arbitrary",), vmem_limit_bytes=VMEM_LIMIT),
        name="experts",
    )(block_e, n_used, xs, w_gu, b_gu.reshape(n_exp, 1, d_gu), w_down,
      b_down.reshape(n_exp, 1, d_model))


def _combine_kernel(tt, n_tiles, dcur_ref, dnxt_ref, gate_ref, h1_ref, p_ref, ys_ref,
                    gple_ref, wgate_ref, wproj_ref, gfin_ref, y_ref, buf, sem):
    i = pl.program_id(0)
    slot = lax.rem(i, 2)

    def issue(dref, sl):
        def body(t, c):
            for k in range(TOP_K):
                src = dref[k * tt + t]
                pltpu.make_async_copy(ys_ref.at[pl.ds(src, 1)], buf.at[sl, k, pl.ds(t, 1)],
                                      sem.at[sl]).start()
            return c
        lax.fori_loop(0, tt, body, 0)

    @pl.when(i == 0)
    def _():
        issue(dcur_ref, 0)

    @pl.when(i + 1 < n_tiles)
    def _():
        issue(dnxt_ref, 1 - slot)

    pltpu.make_async_copy(buf.at[slot], buf.at[slot], sem.at[slot]).wait()

    g = gate_ref[...]
    moe = g[:, 0:1] * buf[slot, 0]
    for k in range(1, TOP_K):
        moe = moe + g[:, k:k + 1] * buf[slot, k]
    h2 = h1_ref[...] + moe
    ple_gate = jax.nn.sigmoid(_dot(_rms(h2, gple_ref[...]).astype(BF16), wgate_ref[...]))
    proj = _dot(p_ref[...].astype(BF16), wproj_ref[...])
    h3 = h2 + proj * ple_gate
    y_ref[...] = _rms(h3, gfin_ref[...])


def _combine(dest_flat, gates_t, h1, p, ys, w, tt):
    tok, d_model = h1.shape
    d_ple = p.shape[-1]
    n_tiles = tok // tt

    def full(a):
        return pl.BlockSpec(a.shape, lambda i: (0,) * a.ndim)

    weights = [w["g_ple"], w["w_ple_gate"], w["w_ple_proj"], w["g_final"]]
    return pl.pallas_call(
        functools.partial(_combine_kernel, tt, n_tiles),
        grid=(n_tiles,),
        in_specs=[pl.BlockSpec((TOP_K * tt,), lambda i: (i,), memory_space=pltpu.SMEM),
                  pl.BlockSpec((TOP_K * tt,), lambda i: (jnp.minimum(i + 1, n_tiles - 1),),
                               memory_space=pltpu.SMEM),
                  pl.BlockSpec((tt, TOP_K), lambda i: (i, 0)),
                  pl.BlockSpec((tt, d_model), lambda i: (i, 0)),
                  pl.BlockSpec((tt, d_ple), lambda i: (i, 0)),
                  pl.BlockSpec(memory_space=pl.ANY)]
                 + [full(a) for a in weights],
        out_specs=pl.BlockSpec((tt, d_model), lambda i: (i, 0)),
        out_shape=jax.ShapeDtypeStruct((tok, d_model), F32),
        scratch_shapes=[pltpu.VMEM((2, TOP_K, tt, d_model), F32),
                        pltpu.SemaphoreType.DMA((2,))],
        compiler_params=pltpu.CompilerParams(
            dimension_semantics=("arbitrary",), vmem_limit_bytes=VMEM_LIMIT),
        name="combine",
    )(dest_flat, dest_flat, gates_t, h1, p, ys, *weights)


def _tile_major(a, tt):
    k, tok = a.shape
    return a.reshape(k, tok // tt, tt).transpose(1, 0, 2).reshape(-1)


def _pad_state(state, hist):
    bsz, n, ch = state.shape
    return jnp.concatenate([jnp.zeros((bsz, hist - n, ch), state.dtype), state], axis=1)


def _forward(x_prompt, x_sample, p_prompt, p_sample, state_conv, state_pool, lw, g_final,
             mix_rows, expert_rows, combine_rows, dispatch_rows):
    bp, sp, d_model = x_prompt.shape
    bs, ss, _ = x_sample.shape
    n_exp = lw["b_router"].shape[0]
    cch = lw["w_conv"].shape[-1]
    pch = lw["pool_scale"].shape[-1]
    tp, tsamp = bp * sp, bs * ss
    tm = expert_rows

    wr = lw["w_router"]
    wr_hi = wr.astype(BF16)
    wr_lo = (wr - wr_hi.astype(F32)).astype(BF16)
    wr_pad = jnp.concatenate(
        [wr_hi, wr_lo, jnp.zeros((d_model, LANES - 2 * n_exp), BF16)], axis=1)
    mw = dict(g_mix=lw["g_mix"].reshape(1, -1), w_in=lw["w_in"].astype(BF16),
              w_conv=lw["w_conv"], w_pool=lw["w_pool"].astype(BF16),
              pool_scale=lw["pool_scale"].reshape(1, -1), w_out=lw["w_out"].astype(BF16),
              g_ffn=lw["g_ffn"].reshape(1, -1), w_router=wr_pad,
              b_router=lw["b_router"].reshape(-1, 1))
    cw = dict(g_ple=lw["g_ple"].reshape(1, -1), w_ple_gate=lw["w_ple_gate"].astype(BF16),
              w_ple_proj=lw["w_ple_proj"].astype(BF16), g_final=g_final.reshape(1, -1))

    zc = jnp.zeros((bp, CONV_HIST, cch), F32)
    zp = jnp.zeros((bp, POOL_HIST, pch), F32)
    ts_p = min(mix_rows, sp)
    (h1_p, xf_p, e_p, g_p, r_p, c_p, conv_p, pool_p) = _mixer(
        x_prompt, zc, zp, 0, mw, 1, ts_p)
    (h1_s, xf_s, e_s, g_s, r_s, c_s, conv_s, pool_s) = _mixer(
        x_sample, _pad_state(state_conv, CONV_HIST), _pad_state(state_pool, POOL_HIST),
        PAST_LEN, mw, bs, ss)

    cnt_p = c_p[:, 0]
    counts = cnt_p + c_s[:, 0]
    padded = ((counts + tm - 1) // tm) * tm
    pend = jnp.cumsum(padded)
    pstart = pend - padded
    dest_p = pstart[e_p] + r_p
    dest_s = (pstart + cnt_p)[e_s] + r_s
    n_assign = TOP_K * (tp + tsamp)
    n_blk = -(-n_assign // tm) + n_exp
    n_used = (pend[-1] // tm).astype(I32)
    blk = jnp.minimum(jnp.arange(n_blk, dtype=I32), n_used - 1)
    block_e = jnp.minimum(jnp.searchsorted(pend, blk * tm, side="right"), n_exp - 1).astype(I32)
    zblk = jnp.where(padded > counts, pend // tm - 1, -1).astype(I32)

    tt_p = min(dispatch_rows, tp)
    xs = _dispatch(zblk, n_used.reshape(1), _tile_major(dest_p, tt_p), _tile_major(dest_s, tsamp),
                   xf_p.reshape(tp, d_model), xf_s.reshape(tsamp, d_model), n_blk, tm, tt_p)

    ys = _experts(block_e, n_used.reshape(1), xs, lw["w_gu"].astype(BF16), lw["b_gu"],
                  lw["w_down"].astype(BF16), lw["b_down"], tm)

    tc_p = min(combine_rows, tp)
    y_p = _combine(_tile_major(dest_p, tc_p), g_p.T, h1_p.reshape(tp, d_model),
                   p_prompt.reshape(tp, -1), ys, cw, tc_p)
    y_s = _combine(_tile_major(dest_s, tsamp), g_s.T, h1_s.reshape(tsamp, d_model),
                   p_sample.reshape(tsamp, -1), ys, cw, tsamp)
    return (y_p.reshape(bp, sp, d_model), y_s.reshape(bs, ss, d_model),
            conv_p, pool_p, conv_s, pool_s)


def kernel(x_prompt, x_sample, p_prompt, p_sample, state_conv, state_pool, g_mix, w_in, w_conv,
           w_pool, pool_scale, w_out, g_ffn, w_router, b_router, w_gu, b_gu, w_down, b_down,
           g_ple, w_ple_proj, w_ple_gate, g_final):
    assert g_mix.shape[0] == 1, "single-layer step"
    lw = dict(g_mix=g_mix[0], w_in=w_in[0], w_conv=w_conv[0], w_pool=w_pool[0],
              pool_scale=pool_scale[0], w_out=w_out[0], g_ffn=g_ffn[0], w_router=w_router[0],
              b_router=b_router[0], w_gu=w_gu[0], b_gu=b_gu[0], w_down=w_down[0],
              b_down=b_down[0], g_ple=g_ple[0], w_ple_proj=w_ple_proj[0],
              w_ple_gate=w_ple_gate[0])
    y_p, y_s, conv_p, pool_p, conv_s, pool_s = _forward(
        x_prompt, x_sample, p_prompt[0], p_sample[0], state_conv[0], state_pool[0], lw, g_final,
        MIX_ROWS, EXPERT_ROWS, COMBINE_ROWS, DISPATCH_ROWS)
    return (y_p, y_s, conv_p[None], pool_p[None], conv_s[None], pool_s[None])
```

```python
import functools

import jax
import jax.numpy as jnp
from jax import lax
from jax.experimental import pallas as pl
from jax.experimental.pallas import tpu as pltpu

F32 = jnp.float32
BF16 = jnp.bfloat16
I32 = jnp.int32

PAST_LEN = 4096
POOL_WINDOWS = (2, 4, 8, 16)
TOP_K = 4
SWIGLU_LIMIT = 7.0
SWIGLU_ALPHA = 1.702
RMS_EPS = 1e-6

CONV_HIST = 8
POOL_HIST = 16
LANES = 128
VMEM_LIMIT = 56 * 1024 * 1024

MIX_ROWS = 512
EXPERT_ROWS = 512
COMBINE_ROWS = 256
DISPATCH_ROWS = 512


def _rms(x, g):
    return x * lax.rsqrt(jnp.mean(x * x, axis=-1, keepdims=True) + RMS_EPS) * g


def _dot(a, b):
    return jnp.dot(a, b, preferred_element_type=F32)


def _mixer_kernel(start_pos, nb, ts, n_exp,
                  x_ref, convp_ref, poolp_ref, gmix_ref, win_ref, wconv_ref, wpool_ref,
                  pscale_ref, wout_ref, gffn_ref, wr_ref, br_ref, tri_ref,
                  h1_ref, xf_ref, tope_ref, gate_ref, rank_ref, cnt_ref, convn_ref, pooln_ref,
                  ev_ref, eu_ref, carry_ref):
    b = pl.program_id(0)
    s = pl.program_id(1)
    rows = nb * ts
    d_model = x_ref.shape[-1]
    cch = wconv_ref.shape[-1]
    gch = wpool_ref.shape[-1]
    n_grp = wpool_ref.shape[0]

    @pl.when((b == 0) & (s == 0))
    def _():
        carry_ref[...] = jnp.zeros_like(carry_ref)

    @pl.when(s == 0)
    def _():
        ev_ref[:, 0:CONV_HIST, :] = convp_ref[...]
        eu_ref[:, 0:POOL_HIST, :] = poolp_ref[...]

    x = x_ref[...].reshape(rows, d_model)
    xn = _rms(x, gmix_ref[...])
    z = _dot(xn.astype(BF16), win_ref[...])
    bg = z[:, 0:cch]
    v = z[:, cch:2 * cch] * z[:, 2 * cch:3 * cch]
    u = z[:, 3 * cch:]

    w0 = wconv_ref[0:1, :]
    w1 = wconv_ref[1:2, :]
    w2 = wconv_ref[2:3, :]
    pos = start_pos + s * ts + lax.broadcasted_iota(I32, (ts, 1), 0)

    mix_rows = []
    for i in range(nb):
        r0 = i * ts
        v_i = v[r0:r0 + ts]
        u_i = u[r0:r0 + ts]
        ev_ref[i, CONV_HIST:CONV_HIST + ts, :] = v_i
        eu_ref[i, POOL_HIST:POOL_HIST + ts, :] = u_i
        conv = (w0 * ev_ref[i, CONV_HIST - 2:CONV_HIST - 2 + ts, :]
                + w1 * ev_ref[i, CONV_HIST - 1:CONV_HIST - 1 + ts, :]
                + w2 * v_i)
        y_a = bg[r0:r0 + ts] * conv

        e = eu_ref[i]
        sums = []
        cur = e
        shift = 1
        for g in range(n_grp):
            cur = cur + pltpu.roll(cur, shift, axis=0)
            sums.append(cur[POOL_HIST:POOL_HIST + ts, 0:gch])
            if g + 1 < n_grp:
                cur = cur[:, gch:]
            shift *= 2
        yb = []
        for g, w in enumerate(POOL_WINDOWS):
            cnt = jnp.minimum(pos + 1, w).astype(F32)
            diff = sums[g] / cnt - u_i[:, g * gch:(g + 1) * gch]
            yb.append(_dot(diff.astype(BF16), wpool_ref[g]))
        y_b = jnp.concatenate(yb, axis=1) * pscale_ref[...]
        mix_rows.append(jnp.concatenate([y_a, y_b], axis=1))

        convn_ref[i] = ev_ref[i, ts + CONV_HIST - 2:ts + CONV_HIST, :]
        pooln_ref[i] = eu_ref[i, ts + 1:ts + POOL_HIST, :]
        ev_ref[i, 0:CONV_HIST, :] = ev_ref[i, ts:ts + CONV_HIST, :]
        eu_ref[i, 0:POOL_HIST, :] = eu_ref[i, ts:ts + POOL_HIST, :]

    mix = mix_rows[0] if nb == 1 else jnp.concatenate(mix_rows, axis=0)
    h1 = x + _dot(mix.astype(BF16), wout_ref[...])
    h1_ref[...] = h1.reshape(nb, ts, d_model)
    xf = _rms(h1, gffn_ref[...])
    xf_ref[...] = xf.reshape(nb, ts, d_model)

    lg = _dot(xf.astype(BF16), wr_ref[...])
    lgt = lg.T
    logits = lgt[0:n_exp] + lgt[n_exp:2 * n_exp] + br_ref[...]

    iota_e = lax.broadcasted_iota(I32, logits.shape, 0)
    top_l, top_e = [], []
    sel = jnp.zeros(logits.shape, F32)
    cur = logits
    for _ in range(TOP_K):
        m = jnp.max(cur, axis=0, keepdims=True)
        idx = jnp.min(jnp.where(cur == m, iota_e, n_exp), axis=0, keepdims=True)
        hit = iota_e == idx
        top_l.append(m)
        top_e.append(idx)
        sel = jnp.where(hit, 1.0, sel)
        cur = jnp.where(hit, -jnp.inf, cur)
    ex = [jnp.exp(l - top_l[0]) for l in top_l]
    den = ex[0] + ex[1] + ex[2] + ex[3]
    gate_ref[...] = jnp.concatenate([e_ / den for e_ in ex], axis=0)
    tope_ref[...] = jnp.concatenate(top_e, axis=0)

    excl = _dot(sel.astype(BF16), tri_ref[...]) + carry_ref[:, 0:1]
    ranks = [jnp.sum(jnp.where(iota_e == e_, excl, 0.0), axis=0, keepdims=True) for e_ in top_e]
    rank_ref[...] = jnp.concatenate(ranks, axis=0).astype(I32)
    carry = carry_ref[...] + jnp.sum(sel, axis=1, keepdims=True)
    carry_ref[...] = carry
    cnt_ref[...] = carry.astype(I32)


def _mixer(x, conv_prev, pool_prev, start_pos, w, nb, ts):
    bsz, seq, d_model = x.shape
    n_s = seq // ts
    n_b = bsz // nb
    rows = nb * ts
    tok = bsz * seq
    cch = w["w_conv"].shape[-1]
    pch = w["pool_scale"].shape[-1]
    n_exp = w["b_router"].shape[0]
    tri = (lax.broadcasted_iota(I32, (rows, rows), 0)
           < lax.broadcasted_iota(I32, (rows, rows), 1)).astype(BF16)

    def full(a):
        return pl.BlockSpec(a.shape, lambda b, s: (0,) * a.ndim)

    weights = [w["g_mix"], w["w_in"], w["w_conv"], w["w_pool"], w["pool_scale"], w["w_out"],
               w["g_ffn"], w["w_router"], w["b_router"], tri]
    tok_spec = pl.BlockSpec((TOP_K, rows), lambda b, s: (0, b * n_s + s))
    return pl.pallas_call(
        functools.partial(_mixer_kernel, start_pos, nb, ts, n_exp),
        grid=(n_b, n_s),
        in_specs=[pl.BlockSpec((nb, ts, d_model), lambda b, s: (b, s, 0)),
                  pl.BlockSpec((nb, CONV_HIST, cch), lambda b, s: (b, 0, 0)),
                  pl.BlockSpec((nb, POOL_HIST, pch), lambda b, s: (b, 0, 0))]
                 + [full(a) for a in weights],
        out_specs=[pl.BlockSpec((nb, ts, d_model), lambda b, s: (b, s, 0)),
                   pl.BlockSpec((nb, ts, d_model), lambda b, s: (b, s, 0)),
                   tok_spec, tok_spec, tok_spec,
                   pl.BlockSpec((n_exp, LANES), lambda b, s: (0, 0)),
                   pl.BlockSpec((nb, 2, cch), lambda b, s: (b, 0, 0)),
                   pl.BlockSpec((nb, POOL_HIST - 1, pch), lambda b, s: (b, 0, 0))],
        out_shape=[jax.ShapeDtypeStruct((bsz, seq, d_model), F32),
                   jax.ShapeDtypeStruct((bsz, seq, d_model), F32),
                   jax.ShapeDtypeStruct((TOP_K, tok), I32),
                   jax.ShapeDtypeStruct((TOP_K, tok), F32),
                   jax.ShapeDtypeStruct((TOP_K, tok), I32),
                   jax.ShapeDtypeStruct((n_exp, LANES), I32),
                   jax.ShapeDtypeStruct((bsz, 2, cch), F32),
                   jax.ShapeDtypeStruct((bsz, POOL_HIST - 1, pch), F32)],
        scratch_shapes=[pltpu.VMEM((nb, ts + CONV_HIST, cch), F32),
                        pltpu.VMEM((nb, ts + POOL_HIST, pch), F32),
                        pltpu.VMEM((n_exp, LANES), F32)],
        compiler_params=pltpu.CompilerParams(
            dimension_semantics=("arbitrary", "arbitrary"), vmem_limit_bytes=VMEM_LIMIT),
        name="mixer",
    )(x, conv_prev, pool_prev, *weights)


def _dispatch_kernel(tt, tsamp, n_p, tm, n_blk, n_exp,
                     zb_ref, nu_ref, dp_ref, ds_ref, xfp_ref, xfs_ref, xs_ref,
                     zero_ref, sem, zsem):
    i = pl.program_id(0)

    def zero_block(j):
        return pltpu.make_async_copy(zero_ref, xs_ref.at[pl.ds(j * tm, tm)], zsem)

    @pl.when(i == 0)
    def _():
        zero_ref[...] = jnp.zeros_like(zero_ref)

        def each_block(fn):
            def tail(e, c):
                @pl.when(zb_ref[e] >= 0)
                def _():
                    fn(zero_block(zb_ref[e]))
                return c
            lax.fori_loop(0, n_exp, tail, 0)

            def unused(j, c):
                fn(zero_block(j))
                return c
            lax.fori_loop(nu_ref[0], n_blk, unused, 0)

        each_block(lambda cp: cp.start())
        each_block(lambda cp: cp.wait())

    def scatter(d_ref, x_ref, n):
        def body(t, c):
            for k in range(TOP_K):
                d = d_ref[k * n + t]
                pltpu.make_async_copy(x_ref.at[pl.ds(t, 1)], xs_ref.at[pl.ds(d, 1)], sem).start()
            return c
        lax.fori_loop(0, n, body, 0)
        pltpu.make_async_copy(xs_ref.at[pl.ds(0, TOP_K * n)], xs_ref.at[pl.ds(0, TOP_K * n)],
                              sem).wait()

    @pl.when(i < n_p)
    def _():
        scatter(dp_ref, xfp_ref, tt)

    @pl.when(i == n_p)
    def _():
        scatter(ds_ref, xfs_ref, tsamp)


def _dispatch(zblk, n_used, dest_p, dest_s, xf_p, xf_s, n_blk, tm, tt):
    tp, d_model = xf_p.shape
    tsamp = xf_s.shape[0]
    n_p = tp // tt
    n_exp = zblk.shape[0]

    def prompt_map(i, zb, nu):
        return (jnp.minimum(i, n_p - 1),)

    return pl.pallas_call(
        functools.partial(_dispatch_kernel, tt, tsamp, n_p, tm, n_blk, n_exp),
        grid_spec=pltpu.PrefetchScalarGridSpec(
            num_scalar_prefetch=2, grid=(n_p + 1,),
            in_specs=[pl.BlockSpec((TOP_K * tt,), prompt_map, memory_space=pltpu.SMEM),
                      pl.BlockSpec((TOP_K * tsamp,), lambda i, zb, nu: (0,),
                                   memory_space=pltpu.SMEM),
                      pl.BlockSpec((tt, d_model), lambda i, zb, nu: prompt_map(i, zb, nu) + (0,)),
                      pl.BlockSpec((tsamp, d_model), lambda i, zb, nu: (0, 0))],
            out_specs=pl.BlockSpec(memory_space=pl.ANY),
            scratch_shapes=[pltpu.VMEM((tm, d_model), F32),
                            pltpu.SemaphoreType.DMA, pltpu.SemaphoreType.DMA]),
        out_shape=jax.ShapeDtypeStruct((n_blk * tm, d_model), F32),
        compiler_params=pltpu.CompilerParams(dimension_semantics=("arbitrary",)),
        name="dispatch",
    )(zblk, n_used, dest_p, dest_s, xf_p, xf_s)


def _expert_kernel(be_ref, nu_ref, x_ref, wgu_ref, bgu_ref, wd_ref, bd_ref, y_ref):
    del be_ref
    d_exp = wd_ref.shape[1]

    used = pl.program_id(0) < nu_ref[0]

    @pl.when(jnp.logical_not(used))
    def _():
        y_ref[...] = jnp.zeros_like(y_ref)

    @pl.when(used)
    def _():
        gu = _dot(x_ref[...].astype(BF16), wgu_ref[0]) + bgu_ref[0]
        glu = jnp.minimum(gu[:, :d_exp], SWIGLU_LIMIT)
        lin = jnp.clip(gu[:, d_exp:], -SWIGLU_LIMIT, SWIGLU_LIMIT)
        act = (lin + 1.0) * glu * jax.nn.sigmoid(SWIGLU_ALPHA * glu)
        y_ref[...] = _dot(act.astype(BF16), wd_ref[0]) + bd_ref[0]


def _experts(block_e, n_used, xs, w_gu, b_gu, w_down, b_down, tm):
    n_rows, d_model = xs.shape
    n_exp, _, d_gu = w_gu.shape
    d_exp = w_down.shape[1]

    def row_map(i, be, nu):
        return (jnp.minimum(i, nu[0] - 1), 0)

    def w_map(i, be, nu):
        return (be[i], 0, 0)

    return pl.pallas_call(
        _expert_kernel,
        grid_spec=pltpu.PrefetchScalarGridSpec(
            num_scalar_prefetch=2, grid=(n_rows // tm,),
            in_specs=[pl.BlockSpec((tm, d_model), row_map),
                      pl.BlockSpec((1, d_model, d_gu), w_map),
                      pl.BlockSpec((1, 1, d_gu), w_map),
                      pl.BlockSpec((1, d_exp, d_model), w_map),
                      pl.BlockSpec((1, 1, d_model), w_map)],
            out_specs=pl.BlockSpec((tm, d_model), lambda i, be, nu: (i, 0))),
        out_shape=jax.ShapeDtypeStruct((n_rows, d_model), F32),
        compiler_params=pltpu.CompilerParams(
            dimension_semantics=("arbitrary",), vmem_limit_bytes=VMEM_LIMIT),
        name="experts",
    )(block_e, n_used, xs, w_gu, b_gu.reshape(n_exp, 1, d_gu), w_down,
      b_down.reshape(n_exp, 1, d_model))


def _combine_kernel(tt, n_tiles, dcur_ref, dnxt_ref, gate_ref, h1_ref, p_ref, ys_ref,
                    gple_ref, wgate_ref, wproj_ref, gfin_ref, y_ref, buf, sem):
    i = pl.program_id(0)
    slot = lax.rem(i, 2)

    def issue(dref, sl):
        def body(t, c):
            for k in range(TOP_K):
                src = dref[k * tt + t]
                pltpu.make_async_copy(ys_ref.at[pl.ds(src, 1)], buf.at[sl, k, pl.ds(t, 1)],
                                      sem.at[sl]).start()
            return c
        lax.fori_loop(0, tt, body, 0)

    @pl.when(i == 0)
    def _():
        issue(dcur_ref, 0)

    @pl.when(i + 1 < n_tiles)
    def _():
        issue(dnxt_ref, 1 - slot)

    pltpu.make_async_copy(buf.at[slot], buf.at[slot], sem.at[slot]).wait()

    g = gate_ref[...]
    moe = g[:, 0:1] * buf[slot, 0]
    for k in range(1, TOP_K):
        moe = moe + g[:, k:k + 1] * buf[slot, k]
    h2 = h1_ref[...] + moe
    ple_gate = jax.nn.sigmoid(_dot(_rms(h2, gple_ref[...]).astype(BF16), wgate_ref[...]))
    proj = _dot(p_ref[...].astype(BF16), wproj_ref[...])
    h3 = h2 + proj * ple_gate
    y_ref[...] = _rms(h3, gfin_ref[...])


def _combine(dest_flat, gates_t, h1, p, ys, w, tt):
    tok, d_model = h1.shape
    d_ple = p.shape[-1]
    n_tiles = tok // tt

    def full(a):
        return pl.BlockSpec(a.shape, lambda i: (0,) * a.ndim)

    weights = [w["g_ple"], w["w_ple_gate"], w["w_ple_proj"], w["g_final"]]
    return pl.pallas_call(
        functools.partial(_combine_kernel, tt, n_tiles),
        grid=(n_tiles,),
        in_specs=[pl.BlockSpec((TOP_K * tt,), lambda i: (i,), memory_space=pltpu.SMEM),
                  pl.BlockSpec((TOP_K * tt,), lambda i: (jnp.minimum(i + 1, n_tiles - 1),),
                               memory_space=pltpu.SMEM),
                  pl.BlockSpec((tt, TOP_K), lambda i: (i, 0)),
                  pl.BlockSpec((tt, d_model), lambda i: (i, 0)),
                  pl.BlockSpec((tt, d_ple), lambda i: (i, 0)),
                  pl.BlockSpec(memory_space=pl.ANY)]
                 + [full(a) for a in weights],
        out_specs=pl.BlockSpec((tt, d_model), lambda i: (i, 0)),
        out_shape=jax.ShapeDtypeStruct((tok, d_model), F32),
        scratch_shapes=[pltpu.VMEM((2, TOP_K, tt, d_model), F32),
                        pltpu.SemaphoreType.DMA((2,))],
        compiler_params=pltpu.CompilerParams(
            dimension_semantics=("arbitrary",), vmem_limit_bytes=VMEM_LIMIT),
        name="combine",
    )(dest_flat, dest_flat, gates_t, h1, p, ys, *weights)


def _tile_major(a, tt):
    k, tok = a.shape
    return a.reshape(k, tok // tt, tt).transpose(1, 0, 2).reshape(-1)


def _lookup(table, idx):
    n = table.shape[0]
    hit = idx[..., None] == jnp.arange(n, dtype=idx.dtype)
    return jnp.sum(jnp.where(hit, table, 0), axis=-1)


def _pad_state(state, hist):
    bsz, n, ch = state.shape
    return jnp.concatenate([jnp.zeros((bsz, hist - n, ch), state.dtype), state], axis=1)


def _forward(x_prompt, x_sample, p_prompt, p_sample, state_conv, state_pool, lw, g_final,
             mix_rows, expert_rows, combine_rows, dispatch_rows):
    bp, sp, d_model = x_prompt.shape
    bs, ss, _ = x_sample.shape
    n_exp = lw["b_router"].shape[0]
    cch = lw["w_conv"].shape[-1]
    pch = lw["pool_scale"].shape[-1]
    tp, tsamp = bp * sp, bs * ss
    tm = expert_rows

    wr = lw["w_router"]
    wr_hi = wr.astype(BF16)
    wr_lo = (wr - wr_hi.astype(F32)).astype(BF16)
    wr_pad = jnp.concatenate(
        [wr_hi, wr_lo, jnp.zeros((d_model, LANES - 2 * n_exp), BF16)], axis=1)
    mw = dict(g_mix=lw["g_mix"].reshape(1, -1), w_in=lw["w_in"].astype(BF16),
              w_conv=lw["w_conv"], w_pool=lw["w_pool"].astype(BF16),
              pool_scale=lw["pool_scale"].reshape(1, -1), w_out=lw["w_out"].astype(BF16),
              g_ffn=lw["g_ffn"].reshape(1, -1), w_router=wr_pad,
              b_router=lw["b_router"].reshape(-1, 1))
    cw = dict(g_ple=lw["g_ple"].reshape(1, -1), w_ple_gate=lw["w_ple_gate"].astype(BF16),
              w_ple_proj=lw["w_ple_proj"].astype(BF16), g_final=g_final.reshape(1, -1))

    zc = jnp.zeros((bp, CONV_HIST, cch), F32)
    zp = jnp.zeros((bp, POOL_HIST, pch), F32)
    ts_p = min(mix_rows, sp)
    (h1_p, xf_p, e_p, g_p, r_p, c_p, conv_p, pool_p) = _mixer(
        x_prompt, zc, zp, 0, mw, 1, ts_p)
    (h1_s, xf_s, e_s, g_s, r_s, c_s, conv_s, pool_s) = _mixer(
        x_sample, _pad_state(state_conv, CONV_HIST), _pad_state(state_pool, POOL_HIST),
        PAST_LEN, mw, bs, ss)

    cnt_p = c_p[:, 0]
    counts = cnt_p + c_s[:, 0]
    padded = ((counts + tm - 1) // tm) * tm
    pend = jnp.cumsum(padded)
    pstart = pend - padded
    dest_p = _lookup(pstart, e_p) + r_p
    dest_s = _lookup(pstart + cnt_p, e_s) + r_s
    n_assign = TOP_K * (tp + tsamp)
    n_blk = -(-n_assign // tm) + n_exp
    n_used = (pend[-1] // tm).astype(I32)
    blk = jnp.minimum(jnp.arange(n_blk, dtype=I32), n_used - 1)
    block_e = jnp.minimum(jnp.sum(pend[None, :] <= blk[:, None] * tm, axis=1), n_exp - 1).astype(I32)
    zblk = jnp.where(padded > counts, pend // tm - 1, -1).astype(I32)

    tt_p = min(dispatch_rows, tp)
    xs = _dispatch(zblk, n_used.reshape(1), _tile_major(dest_p, tt_p), _tile_major(dest_s, tsamp),
                   xf_p.reshape(tp, d_model), xf_s.reshape(tsamp, d_model), n_blk, tm, tt_p)

    ys = _experts(block_e, n_used.reshape(1), xs, lw["w_gu"].astype(BF16), lw["b_gu"],
                  lw["w_down"].astype(BF16), lw["b_down"], tm)

    tc_p = min(combine_rows, tp)
    y_p = _combine(_tile_major(dest_p, tc_p), g_p.T, h1_p.reshape(tp, d_model),
                   p_prompt.reshape(tp, -1), ys, cw, tc_p)
    y_s = _combine(_tile_major(dest_s, tsamp), g_s.T, h1_s.reshape(tsamp, d_model),
                   p_sample.reshape(tsamp, -1), ys, cw, tsamp)
    return (y_p.reshape(bp, sp, d_model), y_s.reshape(bs, ss, d_model),
            conv_p, pool_p, conv_s, pool_s)


def kernel(x_prompt, x_sample, p_prompt, p_sample, state_conv, state_pool, g_mix, w_in, w_conv,
           w_pool, pool_scale, w_out, g_ffn, w_router, b_router, w_gu, b_gu, w_down, b_down,
           g_ple, w_ple_proj, w_ple_gate, g_final):
    assert g_mix.shape[0] == 1, "single-layer step"
    lw = dict(g_mix=g_mix[0], w_in=w_in[0], w_conv=w_conv[0], w_pool=w_pool[0],
              pool_scale=pool_scale[0], w_out=w_out[0], g_ffn=g_ffn[0], w_router=w_router[0],
              b_router=b_router[0], w_gu=w_gu[0], b_gu=b_gu[0], w_down=w_down[0],
              b_down=b_down[0], g_ple=g_ple[0], w_ple_proj=w_ple_proj[0],
              w_ple_gate=w_ple_gate[0])
    y_p, y_s, conv_p, pool_p, conv_s, pool_s = _forward(
        x_prompt, x_sample, p_prompt[0], p_sample[0], state_conv[0], state_pool[0], lw, g_final,
        MIX_ROWS, EXPERT_ROWS, COMBINE_ROWS, DISPATCH_ROWS)
    return (y_p, y_s, conv_p[None], pool_p[None], conv_s[None], pool_s[None])
```

```python
import functools

import jax
import jax.numpy as jnp
from jax import lax
from jax.experimental import pallas as pl
from jax.experimental.pallas import tpu as pltpu

F32 = jnp.float32
BF16 = jnp.bfloat16
I32 = jnp.int32

PAST_LEN = 4096
POOL_WINDOWS = (2, 4, 8, 16)
TOP_K = 4
SWIGLU_LIMIT = 7.0
SWIGLU_ALPHA = 1.702
RMS_EPS = 1e-6

CONV_HIST = 8
POOL_HIST = 16
LANES = 128
VMEM_LIMIT = 56 * 1024 * 1024

MIX_ROWS = 512
EXPERT_ROWS = 512
COMBINE_ROWS = 256
DISPATCH_ROWS = 512
WEIGHT_CAST_ROWS = 128


def _rms(x, g):
    return x * lax.rsqrt(jnp.mean(x * x, axis=-1, keepdims=True) + RMS_EPS) * g


def _dot(a, b):
    return jnp.dot(a, b, preferred_element_type=F32)


def _mixer_kernel(start_pos, nb, ts, n_exp,
                  x_ref, convp_ref, poolp_ref, gmix_ref, win_ref, wconv_ref, wpool_ref,
                  pscale_ref, wout_ref, gffn_ref, wr_ref, br_ref, tri_ref,
                  h1_ref, xf_ref, tope_ref, gate_ref, rank_ref, cnt_ref, convn_ref, pooln_ref,
                  ev_ref, eu_ref, carry_ref):
    b = pl.program_id(0)
    s = pl.program_id(1)
    rows = nb * ts
    d_model = x_ref.shape[-1]
    cch = wconv_ref.shape[-1]
    gch = wpool_ref.shape[-1]
    n_grp = wpool_ref.shape[0]

    @pl.when((b == 0) & (s == 0))
    def _():
        carry_ref[...] = jnp.zeros_like(carry_ref)

    @pl.when(s == 0)
    def _():
        ev_ref[:, 0:CONV_HIST, :] = convp_ref[...]
        eu_ref[:, 0:POOL_HIST, :] = poolp_ref[...]

    x = x_ref[...].reshape(rows, d_model)
    xn = _rms(x, gmix_ref[...])
    z = _dot(xn.astype(BF16), win_ref[...])
    bg = z[:, 0:cch]
    v = z[:, cch:2 * cch] * z[:, 2 * cch:3 * cch]
    u = z[:, 3 * cch:]

    w0 = wconv_ref[0:1, :]
    w1 = wconv_ref[1:2, :]
    w2 = wconv_ref[2:3, :]
    pos = start_pos + s * ts + lax.broadcasted_iota(I32, (ts, 1), 0)

    mix_rows = []
    for i in range(nb):
        r0 = i * ts
        v_i = v[r0:r0 + ts]
        u_i = u[r0:r0 + ts]
        ev_ref[i, CONV_HIST:CONV_HIST + ts, :] = v_i
        eu_ref[i, POOL_HIST:POOL_HIST + ts, :] = u_i
        conv = (w0 * ev_ref[i, CONV_HIST - 2:CONV_HIST - 2 + ts, :]
                + w1 * ev_ref[i, CONV_HIST - 1:CONV_HIST - 1 + ts, :]
                + w2 * v_i)
        y_a = bg[r0:r0 + ts] * conv

        e = eu_ref[i]
        sums = []
        cur = e
        shift = 1
        for g in range(n_grp):
            cur = cur + pltpu.roll(cur, shift, axis=0)
            sums.append(cur[POOL_HIST:POOL_HIST + ts, 0:gch])
            if g + 1 < n_grp:
                cur = cur[:, gch:]
            shift *= 2
        yb = []
        for g, w in enumerate(POOL_WINDOWS):
            cnt = jnp.minimum(pos + 1, w).astype(F32)
            diff = sums[g] / cnt - u_i[:, g * gch:(g + 1) * gch]
            yb.append(_dot(diff.astype(BF16), wpool_ref[g]))
        y_b = jnp.concatenate(yb, axis=1) * pscale_ref[...]
        mix_rows.append(jnp.concatenate([y_a, y_b], axis=1))

        convn_ref[i] = ev_ref[i, ts + CONV_HIST - 2:ts + CONV_HIST, :]
        pooln_ref[i] = eu_ref[i, ts + 1:ts + POOL_HIST, :]
        ev_ref[i, 0:CONV_HIST, :] = ev_ref[i, ts:ts + CONV_HIST, :]
        eu_ref[i, 0:POOL_HIST, :] = eu_ref[i, ts:ts + POOL_HIST, :]

    mix = mix_rows[0] if nb == 1 else jnp.concatenate(mix_rows, axis=0)
    h1 = x + _dot(mix.astype(BF16), wout_ref[...])
    h1_ref[...] = h1.reshape(nb, ts, d_model)
    xf = _rms(h1, gffn_ref[...])
    xf_ref[...] = xf.reshape(nb, ts, d_model)

    lg = _dot(xf.astype(BF16), wr_ref[...])
    lgt = lg.T
    logits = lgt[0:n_exp] + lgt[n_exp:2 * n_exp] + br_ref[...]

    iota_e = lax.broadcasted_iota(I32, logits.shape, 0)
    top_l, top_e = [], []
    sel = jnp.zeros(logits.shape, F32)
    cur = logits
    for _ in range(TOP_K):
        m = jnp.max(cur, axis=0, keepdims=True)
        idx = jnp.min(jnp.where(cur == m, iota_e, n_exp), axis=0, keepdims=True)
        hit = iota_e == idx
        top_l.append(m)
        top_e.append(idx)
        sel = jnp.where(hit, 1.0, sel)
        cur = jnp.where(hit, -jnp.inf, cur)
    ex = [jnp.exp(l - top_l[0]) for l in top_l]
    den = ex[0] + ex[1] + ex[2] + ex[3]
    gate_ref[...] = jnp.concatenate([e_ / den for e_ in ex], axis=0)
    tope_ref[...] = jnp.concatenate(top_e, axis=0)

    excl = _dot(sel.astype(BF16), tri_ref[...]) + carry_ref[:, 0:1]
    ranks = [jnp.sum(jnp.where(iota_e == e_, excl, 0.0), axis=0, keepdims=True) for e_ in top_e]
    rank_ref[...] = jnp.concatenate(ranks, axis=0).astype(I32)
    carry = carry_ref[...] + jnp.sum(sel, axis=1, keepdims=True)
    carry_ref[...] = carry
    cnt_ref[...] = carry.astype(I32)


def _mixer(x, conv_prev, pool_prev, start_pos, w, nb, ts):
    bsz, seq, d_model = x.shape
    n_s = seq // ts
    n_b = bsz // nb
    rows = nb * ts
    tok = bsz * seq
    cch = w["w_conv"].shape[-1]
    pch = w["pool_scale"].shape[-1]
    n_exp = w["b_router"].shape[0]
    tri = (lax.broadcasted_iota(I32, (rows, rows), 0)
           < lax.broadcasted_iota(I32, (rows, rows), 1)).astype(BF16)

    def full(a):
        return pl.BlockSpec(a.shape, lambda b, s: (0,) * a.ndim)

    weights = [w["g_mix"], w["w_in"], w["w_conv"], w["w_pool"], w["pool_scale"], w["w_out"],
               w["g_ffn"], w["w_router"], w["b_router"], tri]
    tok_spec = pl.BlockSpec((TOP_K, rows), lambda b, s: (0, b * n_s + s))
    return pl.pallas_call(
        functools.partial(_mixer_kernel, start_pos, nb, ts, n_exp),
        grid=(n_b, n_s),
        in_specs=[pl.BlockSpec((nb, ts, d_model), lambda b, s: (b, s, 0)),
                  pl.BlockSpec((nb, CONV_HIST, cch), lambda b, s: (b, 0, 0)),
                  pl.BlockSpec((nb, POOL_HIST, pch), lambda b, s: (b, 0, 0))]
                 + [full(a) for a in weights],
        out_specs=[pl.BlockSpec((nb, ts, d_model), lambda b, s: (b, s, 0)),
                   pl.BlockSpec((nb, ts, d_model), lambda b, s: (b, s, 0)),
                   tok_spec, tok_spec, tok_spec,
                   pl.BlockSpec((n_exp, LANES), lambda b, s: (0, 0)),
                   pl.BlockSpec((nb, 2, cch), lambda b, s: (b, 0, 0)),
                   pl.BlockSpec((nb, POOL_HIST - 1, pch), lambda b, s: (b, 0, 0))],
        out_shape=[jax.ShapeDtypeStruct((bsz, seq, d_model), F32),
                   jax.ShapeDtypeStruct((bsz, seq, d_model), F32),
                   jax.ShapeDtypeStruct((TOP_K, tok), I32),
                   jax.ShapeDtypeStruct((TOP_K, tok), F32),
                   jax.ShapeDtypeStruct((TOP_K, tok), I32),
                   jax.ShapeDtypeStruct((n_exp, LANES), I32),
                   jax.ShapeDtypeStruct((bsz, 2, cch), F32),
                   jax.ShapeDtypeStruct((bsz, POOL_HIST - 1, pch), F32)],
        scratch_shapes=[pltpu.VMEM((nb, ts + CONV_HIST, cch), F32),
                        pltpu.VMEM((nb, ts + POOL_HIST, pch), F32),
                        pltpu.VMEM((n_exp, LANES), F32)],
        compiler_params=pltpu.CompilerParams(
            dimension_semantics=("arbitrary", "arbitrary"), vmem_limit_bytes=VMEM_LIMIT),
        name="mixer",
    )(x, conv_prev, pool_prev, *weights)


def _dispatch_kernel(tt, tsamp, n_p, tm, n_blk, n_exp,
                     zb_ref, nu_ref, dp_ref, ds_ref, xfp_ref, xfs_ref, xs_ref,
                     zero_ref, sem, zsem):
    i = pl.program_id(0)

    def zero_block(j):
        return pltpu.make_async_copy(zero_ref, xs_ref.at[pl.ds(j * tm, tm)], zsem)

    @pl.when(i == 0)
    def _():
        zero_ref[...] = jnp.zeros_like(zero_ref)

        def each_block(fn):
            def tail(e, c):
                @pl.when(zb_ref[e] >= 0)
                def _():
                    fn(zero_block(zb_ref[e]))
                return c
            lax.fori_loop(0, n_exp, tail, 0)

            def unused(j, c):
                fn(zero_block(j))
                return c
            lax.fori_loop(nu_ref[0], n_blk, unused, 0)

        each_block(lambda cp: cp.start())
        each_block(lambda cp: cp.wait())

    def scatter(d_ref, x_ref, n):
        def body(t, c):
            for k in range(TOP_K):
                d = d_ref[k * n + t]
                pltpu.make_async_copy(x_ref.at[pl.ds(t, 1)], xs_ref.at[pl.ds(d, 1)],
                                      sem).start(priority=k % 2)
            return c
        lax.fori_loop(0, n, body, 0)
        pltpu.make_async_copy(xs_ref.at[pl.ds(0, TOP_K * n)], xs_ref.at[pl.ds(0, TOP_K * n)],
                              sem).wait()

    @pl.when(i < n_p)
    def _():
        scatter(dp_ref, xfp_ref, tt)

    @pl.when(i == n_p)
    def _():
        scatter(ds_ref, xfs_ref, tsamp)


def _dispatch(zblk, n_used, dest_p, dest_s, xf_p, xf_s, n_blk, tm, tt):
    tp, d_model = xf_p.shape
    tsamp = xf_s.shape[0]
    n_p = tp // tt
    n_exp = zblk.shape[0]

    def prompt_map(i, zb, nu):
        return (jnp.minimum(i, n_p - 1),)

    return pl.pallas_call(
        functools.partial(_dispatch_kernel, tt, tsamp, n_p, tm, n_blk, n_exp),
        grid_spec=pltpu.PrefetchScalarGridSpec(
            num_scalar_prefetch=2, grid=(n_p + 1,),
            in_specs=[pl.BlockSpec((TOP_K * tt,), prompt_map, memory_space=pltpu.SMEM),
                      pl.BlockSpec((TOP_K * tsamp,), lambda i, zb, nu: (0,),
                                   memory_space=pltpu.SMEM),
                      pl.BlockSpec((tt, d_model), lambda i, zb, nu: prompt_map(i, zb, nu) + (0,)),
                      pl.BlockSpec((tsamp, d_model), lambda i, zb, nu: (0, 0))],
            out_specs=pl.BlockSpec(memory_space=pl.ANY),
            scratch_shapes=[pltpu.VMEM((tm, d_model), F32),
                            pltpu.SemaphoreType.DMA, pltpu.SemaphoreType.DMA]),
        out_shape=jax.ShapeDtypeStruct((n_blk * tm, d_model), F32),
        compiler_params=pltpu.CompilerParams(dimension_semantics=("arbitrary",)),
        name="dispatch",
    )(zblk, n_used, dest_p, dest_s, xf_p, xf_s)


def _cast_rows(src_ref, dst_ref, chunk):
    def body(c, carry):
        r = pl.multiple_of(c * chunk, chunk)
        dst_ref[pl.ds(r, chunk), :] = src_ref[0, pl.ds(r, chunk), :].astype(BF16)
        return carry
    lax.fori_loop(0, src_ref.shape[1] // chunk, body, 0)


def _expert_kernel(be_ref, nu_ref, x_ref, wgu_ref, bgu_ref, wd_ref, bd_ref, y_ref,
                   wgu_bf, wd_bf):
    i = pl.program_id(0)
    d_exp = wd_ref.shape[1]
    used = i < nu_ref[0]
    new_expert = (i == 0) | (be_ref[i] != be_ref[jnp.maximum(i - 1, 0)])

    @pl.when(jnp.logical_not(used))
    def _():
        y_ref[...] = jnp.zeros_like(y_ref)

    @pl.when(used & new_expert)
    def _():
        _cast_rows(wgu_ref, wgu_bf, WEIGHT_CAST_ROWS)
        _cast_rows(wd_ref, wd_bf, WEIGHT_CAST_ROWS)

    @pl.when(used)
    def _():
        gu = _dot(x_ref[...].astype(BF16), wgu_bf[...]) + bgu_ref[0]
        glu = jnp.minimum(gu[:, :d_exp], SWIGLU_LIMIT)
        lin = jnp.clip(gu[:, d_exp:], -SWIGLU_LIMIT, SWIGLU_LIMIT)
        act = (lin + 1.0) * glu * jax.nn.sigmoid(SWIGLU_ALPHA * glu)
        y_ref[...] = _dot(act.astype(BF16), wd_bf[...]) + bd_ref[0]


def _experts(block_e, n_used, xs, w_gu, b_gu, w_down, b_down, tm):
    n_rows, d_model = xs.shape
    n_exp, _, d_gu = w_gu.shape
    d_exp = w_down.shape[1]

    def row_map(i, be, nu):
        return (jnp.minimum(i, nu[0] - 1), 0)

    def w_map(i, be, nu):
        return (be[i], 0, 0)

    return pl.pallas_call(
        _expert_kernel,
        grid_spec=pltpu.PrefetchScalarGridSpec(
            num_scalar_prefetch=2, grid=(n_rows // tm,),
            in_specs=[pl.BlockSpec((tm, d_model), row_map),
                      pl.BlockSpec((1, d_model, d_gu), w_map),
                      pl.BlockSpec((1, 1, d_gu), w_map),
                      pl.BlockSpec((1, d_exp, d_model), w_map),
                      pl.BlockSpec((1, 1, d_model), w_map)],
            out_specs=pl.BlockSpec((tm, d_model), lambda i, be, nu: (i, 0)),
            scratch_shapes=[pltpu.VMEM((d_model, d_gu), BF16),
                            pltpu.VMEM((d_exp, d_model), BF16)]),
        out_shape=jax.ShapeDtypeStruct((n_rows, d_model), F32),
        compiler_params=pltpu.CompilerParams(
            dimension_semantics=("arbitrary",), vmem_limit_bytes=VMEM_LIMIT),
        name="experts",
    )(block_e, n_used, xs, w_gu, b_gu.reshape(n_exp, 1, d_gu), w_down,
      b_down.reshape(n_exp, 1, d_model))


def _combine_kernel(tt, n_tiles, dcur_ref, dnxt_ref, gate_ref, h1_ref, p_ref, ys_ref,
                    gple_ref, wgate_ref, wproj_ref, gfin_ref, y_ref, buf, sem):
    i = pl.program_id(0)
    slot = lax.rem(i, 2)

    def issue(dref, sl):
        def body(t, c):
            for k in range(TOP_K):
                src = dref[k * tt + t]
                pltpu.make_async_copy(ys_ref.at[pl.ds(src, 1)], buf.at[sl, k, pl.ds(t, 1)],
                                      sem.at[sl]).start(priority=k % 2)
            return c
        lax.fori_loop(0, tt, body, 0)

    @pl.when(i == 0)
    def _():
        issue(dcur_ref, 0)

    @pl.when(i + 1 < n_tiles)
    def _():
        issue(dnxt_ref, 1 - slot)

    pltpu.make_async_copy(buf.at[slot], buf.at[slot], sem.at[slot]).wait()

    g = gate_ref[...]
    moe = g[:, 0:1] * buf[slot, 0]
    for k in range(1, TOP_K):
        moe = moe + g[:, k:k + 1] * buf[slot, k]
    h2 = h1_ref[...] + moe
    ple_gate = jax.nn.sigmoid(_dot(_rms(h2, gple_ref[...]).astype(BF16), wgate_ref[...]))
    proj = _dot(p_ref[...].astype(BF16), wproj_ref[...])
    h3 = h2 + proj * ple_gate
    y_ref[...] = _rms(h3, gfin_ref[...])


def _combine(dest_flat, gates_t, h1, p, ys, w, tt):
    tok, d_model = h1.shape
    d_ple = p.shape[-1]
    n_tiles = tok // tt

    def full(a):
        return pl.BlockSpec(a.shape, lambda i: (0,) * a.ndim)

    weights = [w["g_ple"], w["w_ple_gate"], w["w_ple_proj"], w["g_final"]]
    return pl.pallas_call(
        functools.partial(_combine_kernel, tt, n_tiles),
        grid=(n_tiles,),
        in_specs=[pl.BlockSpec((TOP_K * tt,), lambda i: (i,), memory_space=pltpu.SMEM),
                  pl.BlockSpec((TOP_K * tt,), lambda i: (jnp.minimum(i + 1, n_tiles - 1),),
                               memory_space=pltpu.SMEM),
                  pl.BlockSpec((tt, TOP_K), lambda i: (i, 0)),
                  pl.BlockSpec((tt, d_model), lambda i: (i, 0)),
                  pl.BlockSpec((tt, d_ple), lambda i: (i, 0)),
                  pl.BlockSpec(memory_space=pl.ANY)]
                 + [full(a) for a in weights],
        out_specs=pl.BlockSpec((tt, d_model), lambda i: (i, 0)),
        out_shape=jax.ShapeDtypeStruct((tok, d_model), F32),
        scratch_shapes=[pltpu.VMEM((2, TOP_K, tt, d_model), F32),
                        pltpu.SemaphoreType.DMA((2,))],
        compiler_params=pltpu.CompilerParams(
            dimension_semantics=("arbitrary",), vmem_limit_bytes=VMEM_LIMIT),
        name="combine",
    )(dest_flat, dest_flat, gates_t, h1, p, ys, *weights)


def _tile_major(a, tt):
    k, tok = a.shape
    return a.reshape(k, tok // tt, tt).transpose(1, 0, 2).reshape(-1)


def _lookup(table, idx):
    n = table.shape[0]
    hit = idx[..., None] == jnp.arange(n, dtype=idx.dtype)
    return jnp.sum(jnp.where(hit, table, 0), axis=-1)


def _pad_state(state, hist):
    bsz, n, ch = state.shape
    return jnp.concatenate([jnp.zeros((bsz, hist - n, ch), state.dtype), state], axis=1)


def _forward(x_prompt, x_sample, p_prompt, p_sample, state_conv, state_pool, lw, g_final,
             mix_rows, expert_rows, combine_rows, dispatch_rows):
    bp, sp, d_model = x_prompt.shape
    bs, ss, _ = x_sample.shape
    n_exp = lw["b_router"].shape[0]
    cch = lw["w_conv"].shape[-1]
    pch = lw["pool_scale"].shape[-1]
    tp, tsamp = bp * sp, bs * ss
    tm = expert_rows

    wr = lw["w_router"]
    wr_hi = wr.astype(BF16)
    wr_lo = (wr - wr_hi.astype(F32)).astype(BF16)
    wr_pad = jnp.concatenate(
        [wr_hi, wr_lo, jnp.zeros((d_model, LANES - 2 * n_exp), BF16)], axis=1)
    mw = dict(g_mix=lw["g_mix"].reshape(1, -1), w_in=lw["w_in"].astype(BF16),
              w_conv=lw["w_conv"], w_pool=lw["w_pool"].astype(BF16),
              pool_scale=lw["pool_scale"].reshape(1, -1), w_out=lw["w_out"].astype(BF16),
              g_ffn=lw["g_ffn"].reshape(1, -1), w_router=wr_pad,
              b_router=lw["b_router"].reshape(-1, 1))
    cw = dict(g_ple=lw["g_ple"].reshape(1, -1), w_ple_gate=lw["w_ple_gate"].astype(BF16),
              w_ple_proj=lw["w_ple_proj"].astype(BF16), g_final=g_final.reshape(1, -1))

    zc = jnp.zeros((bp, CONV_HIST, cch), F32)
    zp = jnp.zeros((bp, POOL_HIST, pch), F32)
    ts_p = min(mix_rows, sp)
    (h1_p, xf_p, e_p, g_p, r_p, c_p, conv_p, pool_p) = _mixer(
        x_prompt, zc, zp, 0, mw, 1, ts_p)
    (h1_s, xf_s, e_s, g_s, r_s, c_s, conv_s, pool_s) = _mixer(
        x_sample, _pad_state(state_conv, CONV_HIST), _pad_state(state_pool, POOL_HIST),
        PAST_LEN, mw, bs, ss)

    cnt_p = c_p[:, 0]
    counts = cnt_p + c_s[:, 0]
    padded = ((counts + tm - 1) // tm) * tm
    pend = jnp.cumsum(padded)
    pstart = pend - padded
    dest_p = _lookup(pstart, e_p) + r_p
    dest_s = _lookup(pstart + cnt_p, e_s) + r_s
    n_assign = TOP_K * (tp + tsamp)
    n_blk = -(-n_assign // tm) + n_exp
    n_used = (pend[-1] // tm).astype(I32)
    blk = jnp.minimum(jnp.arange(n_blk, dtype=I32), n_used - 1)
    block_e = jnp.minimum(jnp.sum(pend[None, :] <= blk[:, None] * tm, axis=1), n_exp - 1).astype(I32)
    zblk = jnp.where(padded > counts, pend // tm - 1, -1).astype(I32)

    tt_p = min(dispatch_rows, tp)
    xs = _dispatch(zblk, n_used.reshape(1), _tile_major(dest_p, tt_p), _tile_major(dest_s, tsamp),
                   xf_p.reshape(tp, d_model), xf_s.reshape(tsamp, d_model), n_blk, tm, tt_p)

    ys = _experts(block_e, n_used.reshape(1), xs, lw["w_gu"], lw["b_gu"],
                  lw["w_down"], lw["b_down"], tm)

    tc_p = min(combine_rows, tp)
    y_p = _combine(_tile_major(dest_p, tc_p), g_p.T, h1_p.reshape(tp, d_model),
                   p_prompt.reshape(tp, -1), ys, cw, tc_p)
    y_s = _combine(_tile_major(dest_s, tsamp), g_s.T, h1_s.reshape(tsamp, d_model),
                   p_sample.reshape(tsamp, -1), ys, cw, tsamp)
    return (y_p.reshape(bp, sp, d_model), y_s.reshape(bs, ss, d_model),
            conv_p, pool_p, conv_s, pool_s)


def kernel(x_prompt, x_sample, p_prompt, p_sample, state_conv, state_pool, g_mix, w_in, w_conv,
           w_pool, pool_scale, w_out, g_ffn, w_router, b_router, w_gu, b_gu, w_down, b_down,
           g_ple, w_ple_proj, w_ple_gate, g_final):
    assert g_mix.shape[0] == 1, "single-layer step"
    lw = dict(g_mix=g_mix[0], w_in=w_in[0], w_conv=w_conv[0], w_pool=w_pool[0],
              pool_scale=pool_scale[0], w_out=w_out[0], g_ffn=g_ffn[0], w_router=w_router[0],
              b_router=b_router[0], w_gu=w_gu[0], b_gu=b_gu[0], w_down=w_down[0],
              b_down=b_down[0], g_ple=g_ple[0], w_ple_proj=w_ple_proj[0],
              w_ple_gate=w_ple_gate[0])
    y_p, y_s, conv_p, pool_p, conv_s, pool_s = _forward(
        x_prompt, x_sample, p_prompt[0], p_sample[0], state_conv[0], state_pool[0], lw, g_final,
        MIX_ROWS, EXPERT_ROWS, COMBINE_ROWS, DISPATCH_ROWS)
    return (y_p, y_s, conv_p[None], pool_p[None], conv_s[None], pool_s[None])
```

```python
import functools

import jax
import jax.numpy as jnp
from jax import lax
from jax.experimental import pallas as pl
from jax.experimental.pallas import tpu as pltpu

F32 = jnp.float32
BF16 = jnp.bfloat16
I32 = jnp.int32

PAST_LEN = 4096
POOL_WINDOWS = (2, 4, 8, 16)
TOP_K = 4
SWIGLU_LIMIT = 7.0
SWIGLU_ALPHA = 1.702
RMS_EPS = 1e-6

CONV_HIST = 8
POOL_HIST = 16
LANES = 128
ROW_TILE = 8
VMEM_LIMIT = 56 * 1024 * 1024

MIX_ROWS = 512
EXPERT_ROWS = 512
COMBINE_ROWS = 256
DISPATCH_ROWS = 512
WEIGHT_CAST_ROWS = 128
ISSUE_UNROLL = 4


def _rms(x, g):
    return x * lax.rsqrt(jnp.mean(x * x, axis=-1, keepdims=True) + RMS_EPS) * g


def _dot(a, b):
    return jnp.dot(a, b, preferred_element_type=F32)


def _store_token_tiles(ref, x):
    n = x.shape[0]
    for j in range(ROW_TILE):
        ref[pl.ds(j, n, stride=ROW_TILE), :] = x[:, j * LANES:(j + 1) * LANES]


def _load_token_tiles(ref, n):
    return jnp.concatenate(
        [ref[pl.ds(j, n, stride=ROW_TILE), :] for j in range(ROW_TILE)], axis=1)


def _mixer_kernel(start_pos, nb, ts, n_exp,
                  x_ref, convp_ref, poolp_ref, gmix_ref, win_ref, wconv_ref, wpool_ref,
                  pscale_ref, wout_ref, gffn_ref, wr_ref, br_ref, tri_ref,
                  h1_ref, xf_ref, tope_ref, gate_ref, rank_ref, cnt_ref, convn_ref, pooln_ref,
                  ev_ref, eu_ref, carry_ref):
    b = pl.program_id(0)
    s = pl.program_id(1)
    rows = nb * ts
    d_model = x_ref.shape[-1]
    cch = wconv_ref.shape[-1]
    gch = wpool_ref.shape[-1]
    n_grp = wpool_ref.shape[0]

    @pl.when((b == 0) & (s == 0))
    def _():
        carry_ref[...] = jnp.zeros_like(carry_ref)

    @pl.when(s == 0)
    def _():
        ev_ref[:, 0:CONV_HIST, :] = convp_ref[...]
        eu_ref[:, 0:POOL_HIST, :] = poolp_ref[...]

    x = x_ref[...].reshape(rows, d_model)
    xn = _rms(x, gmix_ref[...])
    z = _dot(xn.astype(BF16), win_ref[...])
    bg = z[:, 0:cch]
    v = z[:, cch:2 * cch] * z[:, 2 * cch:3 * cch]
    u = z[:, 3 * cch:]

    w0 = wconv_ref[0:1, :]
    w1 = wconv_ref[1:2, :]
    w2 = wconv_ref[2:3, :]
    pos = start_pos + s * ts + lax.broadcasted_iota(I32, (ts, 1), 0)

    mix_rows = []
    for i in range(nb):
        r0 = i * ts
        v_i = v[r0:r0 + ts]
        u_i = u[r0:r0 + ts]
        ev_ref[i, CONV_HIST:CONV_HIST + ts, :] = v_i
        eu_ref[i, POOL_HIST:POOL_HIST + ts, :] = u_i
        conv = (w0 * ev_ref[i, CONV_HIST - 2:CONV_HIST - 2 + ts, :]
                + w1 * ev_ref[i, CONV_HIST - 1:CONV_HIST - 1 + ts, :]
                + w2 * v_i)
        y_a = bg[r0:r0 + ts] * conv

        e = eu_ref[i]
        sums = []
        cur = e
        shift = 1
        for g in range(n_grp):
            cur = cur + pltpu.roll(cur, shift, axis=0)
            sums.append(cur[POOL_HIST:POOL_HIST + ts, 0:gch])
            if g + 1 < n_grp:
                cur = cur[:, gch:]
            shift *= 2
        yb = []
        for g, w in enumerate(POOL_WINDOWS):
            cnt = jnp.minimum(pos + 1, w).astype(F32)
            diff = sums[g] / cnt - u_i[:, g * gch:(g + 1) * gch]
            yb.append(_dot(diff.astype(BF16), wpool_ref[g]))
        y_b = jnp.concatenate(yb, axis=1) * pscale_ref[...]
        mix_rows.append(jnp.concatenate([y_a, y_b], axis=1))

        convn_ref[i] = ev_ref[i, ts + CONV_HIST - 2:ts + CONV_HIST, :]
        pooln_ref[i] = eu_ref[i, ts + 1:ts + POOL_HIST, :]
        ev_ref[i, 0:CONV_HIST, :] = ev_ref[i, ts:ts + CONV_HIST, :]
        eu_ref[i, 0:POOL_HIST, :] = eu_ref[i, ts:ts + POOL_HIST, :]

    mix = mix_rows[0] if nb == 1 else jnp.concatenate(mix_rows, axis=0)
    h1 = x + _dot(mix.astype(BF16), wout_ref[...])
    h1_ref[...] = h1.reshape(nb, ts, d_model)
    xf = _rms(h1, gffn_ref[...])
    _store_token_tiles(xf_ref, xf)

    lg = _dot(xf.astype(BF16), wr_ref[...])
    lgt = lg.T
    logits = lgt[0:n_exp] + lgt[n_exp:2 * n_exp] + br_ref[...]

    iota_e = lax.broadcasted_iota(I32, logits.shape, 0)
    top_l, top_e = [], []
    sel = jnp.zeros(logits.shape, F32)
    cur = logits
    for _ in range(TOP_K):
        m = jnp.max(cur, axis=0, keepdims=True)
        idx = jnp.min(jnp.where(cur == m, iota_e, n_exp), axis=0, keepdims=True)
        hit = iota_e == idx
        top_l.append(m)
        top_e.append(idx)
        sel = jnp.where(hit, 1.0, sel)
        cur = jnp.where(hit, -jnp.inf, cur)
    ex = [jnp.exp(l - top_l[0]) for l in top_l]
    den = ex[0] + ex[1] + ex[2] + ex[3]
    gate_ref[...] = jnp.concatenate([e_ / den for e_ in ex], axis=0)
    tope_ref[...] = jnp.concatenate(top_e, axis=0)

    excl = _dot(sel.astype(BF16), tri_ref[...]) + carry_ref[:, 0:1]
    ranks = [jnp.sum(jnp.where(iota_e == e_, excl, 0.0), axis=0, keepdims=True) for e_ in top_e]
    rank_ref[...] = jnp.concatenate(ranks, axis=0).astype(I32)
    carry = carry_ref[...] + jnp.sum(sel, axis=1, keepdims=True)
    carry_ref[...] = carry
    cnt_ref[...] = carry.astype(I32)


def _mixer(x, conv_prev, pool_prev, start_pos, w, nb, ts):
    bsz, seq, d_model = x.shape
    n_s = seq // ts
    n_b = bsz // nb
    rows = nb * ts
    tok = bsz * seq
    cch = w["w_conv"].shape[-1]
    pch = w["pool_scale"].shape[-1]
    n_exp = w["b_router"].shape[0]
    tri = (lax.broadcasted_iota(I32, (rows, rows), 0)
           < lax.broadcasted_iota(I32, (rows, rows), 1)).astype(BF16)

    def full(a):
        return pl.BlockSpec(a.shape, lambda b, s: (0,) * a.ndim)

    weights = [w["g_mix"], w["w_in"], w["w_conv"], w["w_pool"], w["pool_scale"], w["w_out"],
               w["g_ffn"], w["w_router"], w["b_router"], tri]
    tok_spec = pl.BlockSpec((TOP_K, rows), lambda b, s: (0, b * n_s + s))
    return pl.pallas_call(
        functools.partial(_mixer_kernel, start_pos, nb, ts, n_exp),
        grid=(n_b, n_s),
        in_specs=[pl.BlockSpec((nb, ts, d_model), lambda b, s: (b, s, 0)),
                  pl.BlockSpec((nb, CONV_HIST, cch), lambda b, s: (b, 0, 0)),
                  pl.BlockSpec((nb, POOL_HIST, pch), lambda b, s: (b, 0, 0))]
                 + [full(a) for a in weights],
        out_specs=[pl.BlockSpec((nb, ts, d_model), lambda b, s: (b, s, 0)),
                   pl.BlockSpec((rows * ROW_TILE, LANES), lambda b, s: (b * n_s + s, 0)),
                   tok_spec, tok_spec, tok_spec,
                   pl.BlockSpec((n_exp, LANES), lambda b, s: (0, 0)),
                   pl.BlockSpec((nb, 2, cch), lambda b, s: (b, 0, 0)),
                   pl.BlockSpec((nb, POOL_HIST - 1, pch), lambda b, s: (b, 0, 0))],
        out_shape=[jax.ShapeDtypeStruct((bsz, seq, d_model), F32),
                   jax.ShapeDtypeStruct((tok * ROW_TILE, LANES), F32),
                   jax.ShapeDtypeStruct((TOP_K, tok), I32),
                   jax.ShapeDtypeStruct((TOP_K, tok), F32),
                   jax.ShapeDtypeStruct((TOP_K, tok), I32),
                   jax.ShapeDtypeStruct((n_exp, LANES), I32),
                   jax.ShapeDtypeStruct((bsz, 2, cch), F32),
                   jax.ShapeDtypeStruct((bsz, POOL_HIST - 1, pch), F32)],
        scratch_shapes=[pltpu.VMEM((nb, ts + CONV_HIST, cch), F32),
                        pltpu.VMEM((nb, ts + POOL_HIST, pch), F32),
                        pltpu.VMEM((n_exp, LANES), F32)],
        compiler_params=pltpu.CompilerParams(
            dimension_semantics=("arbitrary", "arbitrary"), vmem_limit_bytes=VMEM_LIMIT),
        name="mixer",
    )(x, conv_prev, pool_prev, *weights)


def _dispatch_kernel(tt, tsamp, n_p, tm, n_blk, n_exp,
                     zb_ref, nu_ref, dp_ref, ds_ref, xfp_ref, xfs_ref, xs_ref,
                     zero_ref, sem, zsem):
    i = pl.program_id(0)

    def zero_block(j):
        r = pl.multiple_of(j * (tm * ROW_TILE), tm * ROW_TILE)
        return pltpu.make_async_copy(zero_ref, xs_ref.at[pl.ds(r, tm * ROW_TILE)], zsem)

    @pl.when(i == 0)
    def _():
        zero_ref[...] = jnp.zeros_like(zero_ref)

        def each_block(fn):
            def tail(e, c):
                @pl.when(zb_ref[e] >= 0)
                def _():
                    fn(zero_block(zb_ref[e]))
                return c
            lax.fori_loop(0, n_exp, tail, 0)

            def unused(j, c):
                fn(zero_block(j))
                return c
            lax.fori_loop(nu_ref[0], n_blk, unused, 0)

        each_block(lambda cp: cp.start())
        each_block(lambda cp: cp.wait())

    def scatter(d_ref, x_ref, n):
        def body(t, c):
            src = x_ref.at[pl.ds(pl.multiple_of(t * ROW_TILE, ROW_TILE), ROW_TILE)]
            for k in range(TOP_K):
                d = pl.multiple_of(d_ref[k * n + t] * ROW_TILE, ROW_TILE)
                pltpu.make_async_copy(src, xs_ref.at[pl.ds(d, ROW_TILE)], sem).start()
            return c
        lax.fori_loop(0, n, body, 0)
        rows = TOP_K * n * ROW_TILE
        pltpu.make_async_copy(xs_ref.at[pl.ds(0, rows)], xs_ref.at[pl.ds(0, rows)], sem).wait()

    @pl.when(i < n_p)
    def _():
        scatter(dp_ref, xfp_ref, tt)

    @pl.when(i == n_p)
    def _():
        scatter(ds_ref, xfs_ref, tsamp)


def _dispatch(zblk, n_used, dest_p, dest_s, xf_p, xf_s, n_blk, tm, tt):
    tp = xf_p.shape[0] // ROW_TILE
    tsamp = xf_s.shape[0] // ROW_TILE
    n_p = tp // tt
    n_exp = zblk.shape[0]

    def prompt_map(i, zb, nu):
        return (jnp.minimum(i, n_p - 1),)

    return pl.pallas_call(
        functools.partial(_dispatch_kernel, tt, tsamp, n_p, tm, n_blk, n_exp),
        grid_spec=pltpu.PrefetchScalarGridSpec(
            num_scalar_prefetch=2, grid=(n_p + 1,),
            in_specs=[pl.BlockSpec((TOP_K * tt,), prompt_map, memory_space=pltpu.SMEM),
                      pl.BlockSpec((TOP_K * tsamp,), lambda i, zb, nu: (0,),
                                   memory_space=pltpu.SMEM),
                      pl.BlockSpec((tt * ROW_TILE, LANES),
                                   lambda i, zb, nu: prompt_map(i, zb, nu) + (0,)),
                      pl.BlockSpec((tsamp * ROW_TILE, LANES), lambda i, zb, nu: (0, 0))],
            out_specs=pl.BlockSpec(memory_space=pl.ANY),
            scratch_shapes=[pltpu.VMEM((tm * ROW_TILE, LANES), F32),
                            pltpu.SemaphoreType.DMA, pltpu.SemaphoreType.DMA]),
        out_shape=jax.ShapeDtypeStruct((n_blk * tm * ROW_TILE, LANES), F32),
        compiler_params=pltpu.CompilerParams(dimension_semantics=("arbitrary",)),
        name="dispatch",
    )(zblk, n_used, dest_p, dest_s, xf_p, xf_s)


def _cast_rows(src_ref, dst_ref, chunk):
    def body(c, carry):
        r = pl.multiple_of(c * chunk, chunk)
        dst_ref[pl.ds(r, chunk), :] = src_ref[0, pl.ds(r, chunk), :].astype(BF16)
        return carry
    lax.fori_loop(0, src_ref.shape[1] // chunk, body, 0)


def _expert_kernel(tm, be_ref, nu_ref, x_ref, wgu_ref, bgu_ref, wd_ref, bd_ref, y_ref,
                   wgu_bf, wd_bf):
    i = pl.program_id(0)
    d_exp = wd_ref.shape[1]
    used = i < nu_ref[0]
    new_expert = (i == 0) | (be_ref[i] != be_ref[jnp.maximum(i - 1, 0)])

    @pl.when(jnp.logical_not(used))
    def _():
        y_ref[...] = jnp.zeros_like(y_ref)

    @pl.when(used & new_expert)
    def _():
        _cast_rows(wgu_ref, wgu_bf, WEIGHT_CAST_ROWS)
        _cast_rows(wd_ref, wd_bf, WEIGHT_CAST_ROWS)

    @pl.when(used)
    def _():
        x = _load_token_tiles(x_ref, tm)
        gu = _dot(x.astype(BF16), wgu_bf[...]) + bgu_ref[0]
        glu = jnp.minimum(gu[:, :d_exp], SWIGLU_LIMIT)
        lin = jnp.clip(gu[:, d_exp:], -SWIGLU_LIMIT, SWIGLU_LIMIT)
        act = (lin + 1.0) * glu * jax.nn.sigmoid(SWIGLU_ALPHA * glu)
        _store_token_tiles(y_ref, _dot(act.astype(BF16), wd_bf[...]) + bd_ref[0])


def _experts(block_e, n_used, xs, w_gu, b_gu, w_down, b_down, tm):
    n_exp, d_model, d_gu = w_gu.shape
    d_exp = w_down.shape[1]
    n_blk = xs.shape[0] // (tm * ROW_TILE)

    def row_map(i, be, nu):
        return (jnp.minimum(i, nu[0] - 1), 0)

    def w_map(i, be, nu):
        return (be[i], 0, 0)

    return pl.pallas_call(
        functools.partial(_expert_kernel, tm),
        grid_spec=pltpu.PrefetchScalarGridSpec(
            num_scalar_prefetch=2, grid=(n_blk,),
            in_specs=[pl.BlockSpec((tm * ROW_TILE, LANES), row_map),
                      pl.BlockSpec((1, d_model, d_gu), w_map),
                      pl.BlockSpec((1, 1, d_gu), w_map),
                      pl.BlockSpec((1, d_exp, d_model), w_map),
                      pl.BlockSpec((1, 1, d_model), w_map)],
            out_specs=pl.BlockSpec((tm * ROW_TILE, LANES), lambda i, be, nu: (i, 0)),
            scratch_shapes=[pltpu.VMEM((d_model, d_gu), BF16),
                            pltpu.VMEM((d_exp, d_model), BF16)]),
        out_shape=jax.ShapeDtypeStruct(xs.shape, F32),
        compiler_params=pltpu.CompilerParams(
            dimension_semantics=("arbitrary",), vmem_limit_bytes=VMEM_LIMIT),
        name="experts",
    )(block_e, n_used, xs, w_gu, b_gu.reshape(n_exp, 1, d_gu), w_down,
      b_down.reshape(n_exp, 1, d_model))


def _combine_kernel(tt, n_tiles, dcur_ref, dnxt_ref, gate_ref, h1_ref, p_ref, ys_ref,
                    gple_ref, wgate_ref, wproj_ref, gfin_ref, y_ref, buf, sem):
    i = pl.program_id(0)
    slot = lax.rem(i, 2)

    def issue(dref, sl):
        def body(t, c):
            r = pl.multiple_of(t * ROW_TILE, ROW_TILE)
            for k in range(TOP_K):
                src = pl.multiple_of(dref[k * tt + t] * ROW_TILE, ROW_TILE)
                pltpu.make_async_copy(ys_ref.at[pl.ds(src, ROW_TILE)],
                                      buf.at[sl, k, pl.ds(r, ROW_TILE)], sem.at[sl]).start()
            return c
        lax.fori_loop(0, tt, body, 0, unroll=ISSUE_UNROLL)

    @pl.when(i == 0)
    def _():
        issue(dcur_ref, 0)

    @pl.when(i + 1 < n_tiles)
    def _():
        issue(dnxt_ref, 1 - slot)

    pltpu.make_async_copy(buf.at[slot], buf.at[slot], sem.at[slot]).wait()

    g = gate_ref[...]
    moe = g[:, 0:1] * _load_token_tiles(buf.at[slot, 0], tt)
    for k in range(1, TOP_K):
        moe = moe + g[:, k:k + 1] * _load_token_tiles(buf.at[slot, k], tt)
    h2 = h1_ref[...] + moe
    ple_gate = jax.nn.sigmoid(_dot(_rms(h2, gple_ref[...]).astype(BF16), wgate_ref[...]))
    proj = _dot(p_ref[...].astype(BF16), wproj_ref[...])
    h3 = h2 + proj * ple_gate
    y_ref[...] = _rms(h3, gfin_ref[...])


def _combine(dest_flat, gates_t, h1, p, ys, w, tt):
    tok, d_model = h1.shape
    d_ple = p.shape[-1]
    n_tiles = tok // tt

    def full(a):
        return pl.BlockSpec(a.shape, lambda i: (0,) * a.ndim)

    weights = [w["g_ple"], w["w_ple_gate"], w["w_ple_proj"], w["g_final"]]
    return pl.pallas_call(
        functools.partial(_combine_kernel, tt, n_tiles),
        grid=(n_tiles,),
        in_specs=[pl.BlockSpec((TOP_K * tt,), lambda i: (i,), memory_space=pltpu.SMEM),
                  pl.BlockSpec((TOP_K * tt,), lambda i: (jnp.minimum(i + 1, n_tiles - 1),),
                               memory_space=pltpu.SMEM),
                  pl.BlockSpec((tt, TOP_K), lambda i: (i, 0)),
                  pl.BlockSpec((tt, d_model), lambda i: (i, 0)),
                  pl.BlockSpec((tt, d_ple), lambda i: (i, 0)),
                  pl.BlockSpec(memory_space=pl.ANY)]
                 + [full(a) for a in weights],
        out_specs=pl.BlockSpec((tt, d_model), lambda i: (i, 0)),
        out_shape=jax.ShapeDtypeStruct((tok, d_model), F32),
        scratch_shapes=[pltpu.VMEM((2, TOP_K, tt * ROW_TILE, LANES), F32),
                        pltpu.SemaphoreType.DMA((2,))],
        compiler_params=pltpu.CompilerParams(
            dimension_semantics=("arbitrary",), vmem_limit_bytes=VMEM_LIMIT),
        name="combine",
    )(dest_flat, dest_flat, gates_t, h1, p, ys, *weights)


def _tile_major(a, tt):
    k, tok = a.shape
    return a.reshape(k, tok // tt, tt).transpose(1, 0, 2).reshape(-1)


def _lookup(table, idx):
    n = table.shape[0]
    hit = idx[..., None] == jnp.arange(n, dtype=idx.dtype)
    return jnp.sum(jnp.where(hit, table, 0), axis=-1)


def _pad_state(state, hist):
    bsz, n, ch = state.shape
    return jnp.concatenate([jnp.zeros((bsz, hist - n, ch), state.dtype), state], axis=1)


def _forward(x_prompt, x_sample, p_prompt, p_sample, state_conv, state_pool, lw, g_final,
             mix_rows, expert_rows, combine_rows, dispatch_rows):
    bp, sp, d_model = x_prompt.shape
    bs, ss, _ = x_sample.shape
    assert d_model == ROW_TILE * LANES, "token-tile layout needs a 1024-wide model"
    n_exp = lw["b_router"].shape[0]
    cch = lw["w_conv"].shape[-1]
    pch = lw["pool_scale"].shape[-1]
    tp, tsamp = bp * sp, bs * ss
    tm = expert_rows

    wr = lw["w_router"]
    wr_hi = wr.astype(BF16)
    wr_lo = (wr - wr_hi.astype(F32)).astype(BF16)
    wr_pad = jnp.concatenate(
        [wr_hi, wr_lo, jnp.zeros((d_model, LANES - 2 * n_exp), BF16)], axis=1)
    mw = dict(g_mix=lw["g_mix"].reshape(1, -1), w_in=lw["w_in"].astype(BF16),
              w_conv=lw["w_conv"], w_pool=lw["w_pool"].astype(BF16),
              pool_scale=lw["pool_scale"].reshape(1, -1), w_out=lw["w_out"].astype(BF16),
              g_ffn=lw["g_ffn"].reshape(1, -1), w_router=wr_pad,
              b_router=lw["b_router"].reshape(-1, 1))
    cw = dict(g_ple=lw["g_ple"].reshape(1, -1), w_ple_gate=lw["w_ple_gate"].astype(BF16),
              w_ple_proj=lw["w_ple_proj"].astype(BF16), g_final=g_final.reshape(1, -1))

    zc = jnp.zeros((bp, CONV_HIST, cch), F32)
    zp = jnp.zeros((bp, POOL_HIST, pch), F32)
    ts_p = min(mix_rows, sp)
    (h1_p, xf_p, e_p, g_p, r_p, c_p, conv_p, pool_p) = _mixer(
        x_prompt, zc, zp, 0, mw, 1, ts_p)
    (h1_s, xf_s, e_s, g_s, r_s, c_s, conv_s, pool_s) = _mixer(
        x_sample, _pad_state(state_conv, CONV_HIST), _pad_state(state_pool, POOL_HIST),
        PAST_LEN, mw, bs, ss)

    cnt_p = c_p[:, 0]
    counts = cnt_p + c_s[:, 0]
    padded = ((counts + tm - 1) // tm) * tm
    pend = jnp.cumsum(padded)
    pstart = pend - padded
    dest_p = _lookup(pstart, e_p) + r_p
    dest_s = _lookup(pstart + cnt_p, e_s) + r_s
    n_assign = TOP_K * (tp + tsamp)
    n_blk = -(-n_assign // tm) + n_exp
    n_used = (pend[-1] // tm).astype(I32)
    blk = jnp.minimum(jnp.arange(n_blk, dtype=I32), n_used - 1)
    block_e = jnp.minimum(jnp.sum(pend[None, :] <= blk[:, None] * tm, axis=1), n_exp - 1).astype(I32)
    zblk = jnp.where(padded > counts, pend // tm - 1, -1).astype(I32)

    tt_p = min(dispatch_rows, tp)
    xs = _dispatch(zblk, n_used.reshape(1), _tile_major(dest_p, tt_p), _tile_major(dest_s, tsamp),
                   xf_p, xf_s, n_blk, tm, tt_p)

    ys = _experts(block_e, n_used.reshape(1), xs, lw["w_gu"], lw["b_gu"], lw["w_down"],
                  lw["b_down"], tm)

    tc_p = min(combine_rows, tp)
    y_p = _combine(_tile_major(dest_p, tc_p), g_p.T, h1_p.reshape(tp, d_model),
                   p_prompt.reshape(tp, -1), ys, cw, tc_p)
    y_s = _combine(_tile_major(dest_s, tsamp), g_s.T, h1_s.reshape(tsamp, d_model),
                   p_sample.reshape(tsamp, -1), ys, cw, tsamp)
    return (y_p.reshape(bp, sp, d_model), y_s.reshape(bs, ss, d_model),
            conv_p, pool_p, conv_s, pool_s)


def kernel(x_prompt, x_sample, p_prompt, p_sample, state_conv, state_pool, g_mix, w_in, w_conv,
           w_pool, pool_scale, w_out, g_ffn, w_router, b_router, w_gu, b_gu, w_down, b_down,
           g_ple, w_ple_proj, w_ple_gate, g_final):
    assert g_mix.shape[0] == 1, "single-layer step"
    lw = dict(g_mix=g_mix[0], w_in=w_in[0], w_conv=w_conv[0], w_pool=w_pool[0],
              pool_scale=pool_scale[0], w_out=w_out[0], g_ffn=g_ffn[0], w_router=w_router[0],
              b_router=b_router[0], w_gu=w_gu[0], b_gu=b_gu[0], w_down=w_down[0],
              b_down=b_down[0], g_ple=g_ple[0], w_ple_proj=w_ple_proj[0],
              w_ple_gate=w_ple_gate[0])
    y_p, y_s, conv_p, pool_p, conv_s, pool_s = _forward(
        x_prompt, x_sample, p_prompt[0], p_sample[0], state_conv[0], state_pool[0], lw, g_final,
        MIX_ROWS, EXPERT_ROWS, COMBINE_ROWS, DISPATCH_ROWS)
    return (y_p, y_s, conv_p[None], pool_p[None], conv_s[None], pool_s[None])
```

```python
import functools

import jax
import jax.numpy as jnp
from jax import lax
from jax.experimental import pallas as pl
from jax.experimental.pallas import tpu as pltpu

F32 = jnp.float32
BF16 = jnp.bfloat16
I32 = jnp.int32

PAST_LEN = 4096
POOL_WINDOWS = (2, 4, 8, 16)
TOP_K = 4
SWIGLU_LIMIT = 7.0
SWIGLU_ALPHA = 1.702
RMS_EPS = 1e-6

CONV_HIST = 8
POOL_HIST = 16
LANES = 128
ROW_TILE = 8
VMEM_LIMIT = 56 * 1024 * 1024

MIX_ROWS = 512
EXPERT_ROWS = 512
COMBINE_ROWS = 256
WEIGHT_CAST_ROWS = 128
ISSUE_UNROLL = 4


def _rms(x, g):
    return x * lax.rsqrt(jnp.mean(x * x, axis=-1, keepdims=True) + RMS_EPS) * g


def _dot(a, b):
    return jnp.dot(a, b, preferred_element_type=F32)


def _store_token_tiles(ref, x):
    n = x.shape[0]
    for j in range(ROW_TILE):
        ref[pl.ds(j, n, stride=ROW_TILE), :] = x[:, j * LANES:(j + 1) * LANES]


def _load_token_tiles(ref, n):
    return jnp.concatenate(
        [ref[pl.ds(j, n, stride=ROW_TILE), :] for j in range(ROW_TILE)], axis=1)


def _token_tile(ref, t):
    return ref.at[pl.ds(pl.multiple_of(t * ROW_TILE, ROW_TILE), ROW_TILE)]


MIXER_INPUTS = 14


def _mixer_kernel(start_pos, nb, ts, n_exp, cap, aliased, *refs):
    if aliased:
        refs = refs[:MIXER_INPUTS] + refs[MIXER_INPUTS + 1:]
    (x_ref, convp_ref, poolp_ref, cnt0_ref, gmix_ref, win_ref, wconv_ref, wpool_ref,
     pscale_ref, wout_ref, gffn_ref, wr_ref, br_ref, tri_ref,
     h1_ref, tope_ref, gate_ref, rank_ref, cnt_ref, convn_ref, pooln_ref, xs_ref,
     ev_ref, eu_ref, carry_ref, xf_st, dvm, dsm, sems, dsem) = refs
    b = pl.program_id(0)
    s = pl.program_id(1)
    step = b * pl.num_programs(1) + s
    n_steps = pl.num_programs(0) * pl.num_programs(1)
    slot = lax.rem(step, 2)
    rows = nb * ts
    d_model = x_ref.shape[-1]
    cch = wconv_ref.shape[-1]
    gch = wpool_ref.shape[-1]
    n_grp = wpool_ref.shape[0]

    @pl.when(step == 0)
    def _():
        carry_ref[...] = cnt0_ref[...].astype(F32)

    @pl.when(s == 0)
    def _():
        ev_ref[:, 0:CONV_HIST, :] = convp_ref[...]
        eu_ref[:, 0:POOL_HIST, :] = poolp_ref[...]

    x = x_ref[...].reshape(rows, d_model)
    xn = _rms(x, gmix_ref[...])
    z = _dot(xn.astype(BF16), win_ref[...])
    bg = z[:, 0:cch]
    v = z[:, cch:2 * cch] * z[:, 2 * cch:3 * cch]
    u = z[:, 3 * cch:]

    w0 = wconv_ref[0:1, :]
    w1 = wconv_ref[1:2, :]
    w2 = wconv_ref[2:3, :]
    pos = start_pos + s * ts + lax.broadcasted_iota(I32, (ts, 1), 0)

    mix_rows = []
    for i in range(nb):
        r0 = i * ts
        v_i = v[r0:r0 + ts]
        u_i = u[r0:r0 + ts]
        ev_ref[i, CONV_HIST:CONV_HIST + ts, :] = v_i
        eu_ref[i, POOL_HIST:POOL_HIST + ts, :] = u_i
        conv = (w0 * ev_ref[i, CONV_HIST - 2:CONV_HIST - 2 + ts, :]
                + w1 * ev_ref[i, CONV_HIST - 1:CONV_HIST - 1 + ts, :]
                + w2 * v_i)
        y_a = bg[r0:r0 + ts] * conv

        e = eu_ref[i]
        sums = []
        cur = e
        shift = 1
        for g in range(n_grp):
            cur = cur + pltpu.roll(cur, shift, axis=0)
            sums.append(cur[POOL_HIST:POOL_HIST + ts, 0:gch])
            if g + 1 < n_grp:
                cur = cur[:, gch:]
            shift *= 2
        yb = []
        for g, w in enumerate(POOL_WINDOWS):
            cnt = jnp.minimum(pos + 1, w).astype(F32)
            diff = sums[g] / cnt - u_i[:, g * gch:(g + 1) * gch]
            yb.append(_dot(diff.astype(BF16), wpool_ref[g]))
        y_b = jnp.concatenate(yb, axis=1) * pscale_ref[...]
        mix_rows.append(jnp.concatenate([y_a, y_b], axis=1))

        convn_ref[i] = ev_ref[i, ts + CONV_HIST - 2:ts + CONV_HIST, :]
        pooln_ref[i] = eu_ref[i, ts + 1:ts + POOL_HIST, :]
        ev_ref[i, 0:CONV_HIST, :] = ev_ref[i, ts:ts + CONV_HIST, :]
        eu_ref[i, 0:POOL_HIST, :] = eu_ref[i, ts:ts + POOL_HIST, :]

    mix = mix_rows[0] if nb == 1 else jnp.concatenate(mix_rows, axis=0)
    h1 = x + _dot(mix.astype(BF16), wout_ref[...])
    h1_ref[...] = h1.reshape(nb, ts, d_model)
    xf = _rms(h1, gffn_ref[...])
    _store_token_tiles(xf_st.at[slot], xf)

    lg = _dot(xf.astype(BF16), wr_ref[...])
    lgt = lg.T
    logits = lgt[0:n_exp] + lgt[n_exp:2 * n_exp] + br_ref[...]

    iota_e = lax.broadcasted_iota(I32, logits.shape, 0)
    top_l, top_e = [], []
    sel = jnp.zeros(logits.shape, F32)
    cur = logits
    for _ in range(TOP_K):
        m = jnp.max(cur, axis=0, keepdims=True)
        idx = jnp.min(jnp.where(cur == m, iota_e, n_exp), axis=0, keepdims=True)
        hit = iota_e == idx
        top_l.append(m)
        top_e.append(idx)
        sel = jnp.where(hit, 1.0, sel)
        cur = jnp.where(hit, -jnp.inf, cur)
    ex = [jnp.exp(l - top_l[0]) for l in top_l]
    den = ex[0] + ex[1] + ex[2] + ex[3]
    gate_ref[...] = jnp.concatenate([e_ / den for e_ in ex], axis=0)
    tope = jnp.concatenate(top_e, axis=0)
    tope_ref[...] = tope

    excl = _dot(sel.astype(BF16), tri_ref[...]) + carry_ref[:, 0:1]
    ranks = [jnp.sum(jnp.where(iota_e == e_, excl, 0.0), axis=0, keepdims=True) for e_ in top_e]
    rank = jnp.concatenate(ranks, axis=0).astype(I32)
    rank_ref[...] = rank
    carry = carry_ref[...] + jnp.sum(sel, axis=1, keepdims=True)
    carry_ref[...] = carry
    cnt_ref[...] = carry.astype(I32)

    dvm[...] = tope * cap + rank
    to_smem = pltpu.make_async_copy(dvm, dsm, dsem)
    to_smem.start()
    to_smem.wait()

    def wait_tile(sl):
        n = TOP_K * rows * ROW_TILE
        pltpu.make_async_copy(xs_ref.at[pl.ds(0, n)], xs_ref.at[pl.ds(0, n)], sems.at[sl]).wait()

    @pl.when(step > 0)
    def _():
        wait_tile(1 - slot)

    def issue(t, c):
        src = _token_tile(xf_st.at[slot], t)
        for k in range(TOP_K):
            pltpu.make_async_copy(src, _token_tile(xs_ref, dsm[k, t]), sems.at[slot]).start()
        return c
    lax.fori_loop(0, rows, issue, 0, unroll=ISSUE_UNROLL)

    @pl.when(step == n_steps - 1)
    def _():
        wait_tile(slot)


def _mixer(x, conv_prev, pool_prev, cnt0, xs, xs_rows, cap, start_pos, w, nb, ts):
    bsz, seq, d_model = x.shape
    n_s = seq // ts
    n_b = bsz // nb
    rows = nb * ts
    tok = bsz * seq
    cch = w["w_conv"].shape[-1]
    pch = w["pool_scale"].shape[-1]
    n_exp = w["b_router"].shape[0]
    tri = (lax.broadcasted_iota(I32, (rows, rows), 0)
           < lax.broadcasted_iota(I32, (rows, rows), 1)).astype(BF16)

    def full(a):
        return pl.BlockSpec(a.shape, lambda b, s: (0,) * a.ndim)

    weights = [w["g_mix"], w["w_in"], w["w_conv"], w["w_pool"], w["pool_scale"], w["w_out"],
               w["g_ffn"], w["w_router"], w["b_router"], tri]
    tok_spec = pl.BlockSpec((TOP_K, rows), lambda b, s: (0, b * n_s + s))
    operands = [x, conv_prev, pool_prev, cnt0] + weights
    assert len(operands) == MIXER_INPUTS
    in_specs = ([pl.BlockSpec((nb, ts, d_model), lambda b, s: (b, s, 0)),
                 pl.BlockSpec((nb, CONV_HIST, cch), lambda b, s: (b, 0, 0)),
                 pl.BlockSpec((nb, POOL_HIST, pch), lambda b, s: (b, 0, 0)),
                 full(cnt0)] + [full(a) for a in weights])
    aliases = {}
    if xs is not None:
        operands.append(xs)
        in_specs.append(pl.BlockSpec(memory_space=pl.ANY))
        aliases = {MIXER_INPUTS: 7}
    return pl.pallas_call(
        functools.partial(_mixer_kernel, start_pos, nb, ts, n_exp, cap, xs is not None),
        grid=(n_b, n_s),
        in_specs=in_specs,
        out_specs=[pl.BlockSpec((nb, ts, d_model), lambda b, s: (b, s, 0)),
                   tok_spec, tok_spec, tok_spec,
                   pl.BlockSpec((n_exp, LANES), lambda b, s: (0, 0)),
                   pl.BlockSpec((nb, 2, cch), lambda b, s: (b, 0, 0)),
                   pl.BlockSpec((nb, POOL_HIST - 1, pch), lambda b, s: (b, 0, 0)),
                   pl.BlockSpec(memory_space=pl.ANY)],
        out_shape=[jax.ShapeDtypeStruct((bsz, seq, d_model), F32),
                   jax.ShapeDtypeStruct((TOP_K, tok), I32),
                   jax.ShapeDtypeStruct((TOP_K, tok), F32),
                   jax.ShapeDtypeStruct((TOP_K, tok), I32),
                   jax.ShapeDtypeStruct((n_exp, LANES), I32),
                   jax.ShapeDtypeStruct((bsz, 2, cch), F32),
                   jax.ShapeDtypeStruct((bsz, POOL_HIST - 1, pch), F32),
                   jax.ShapeDtypeStruct((xs_rows * ROW_TILE, LANES), F32)],
        scratch_shapes=[pltpu.VMEM((nb, ts + CONV_HIST, cch), F32),
                        pltpu.VMEM((nb, ts + POOL_HIST, pch), F32),
                        pltpu.VMEM((n_exp, LANES), F32),
                        pltpu.VMEM((2, rows * ROW_TILE, LANES), F32),
                        pltpu.VMEM((TOP_K, rows), I32),
                        pltpu.SMEM((TOP_K, rows), I32),
                        pltpu.SemaphoreType.DMA((2,)),
                        pltpu.SemaphoreType.DMA],
        input_output_aliases=aliases,
        compiler_params=pltpu.CompilerParams(
            dimension_semantics=("arbitrary", "arbitrary"), vmem_limit_bytes=VMEM_LIMIT),
        name="mixer",
    )(*operands)


def _tails_kernel(n_exp, zs_ref, zn_ref, xs_in_ref, xs_ref, zero_ref, sem):
    del xs_in_ref
    zero_ref[...] = jnp.zeros_like(zero_ref)

    def each_row(fn):
        def per_expert(e, c):
            def per_row(r, c2):
                fn(pltpu.make_async_copy(zero_ref, _token_tile(xs_ref, zs_ref[e] + r), sem))
                return c2
            lax.fori_loop(0, zn_ref[e], per_row, 0)
            return c
        lax.fori_loop(0, n_exp, per_expert, 0)

    each_row(lambda cp: cp.start())
    each_row(lambda cp: cp.wait())


def _zero_tails(zstart, zcount, xs):
    return pl.pallas_call(
        functools.partial(_tails_kernel, zstart.shape[0]),
        grid_spec=pltpu.PrefetchScalarGridSpec(
            num_scalar_prefetch=2, grid=(1,),
            in_specs=[pl.BlockSpec(memory_space=pl.ANY)],
            out_specs=pl.BlockSpec(memory_space=pl.ANY),
            scratch_shapes=[pltpu.VMEM((ROW_TILE, LANES), F32), pltpu.SemaphoreType.DMA]),
        out_shape=jax.ShapeDtypeStruct(xs.shape, xs.dtype),
        input_output_aliases={2: 0},
        compiler_params=pltpu.CompilerParams(dimension_semantics=("arbitrary",)),
        name="zero_tails",
    )(zstart, zcount, xs)


def _cast_rows(src_ref, dst_ref, chunk):
    def body(c, carry):
        r = pl.multiple_of(c * chunk, chunk)
        dst_ref[pl.ds(r, chunk), :] = src_ref[0, pl.ds(r, chunk), :].astype(BF16)
        return carry
    lax.fori_loop(0, src_ref.shape[1] // chunk, body, 0)


def _expert_kernel(tm, be_ref, br_ref, nu_ref, x_ref, wgu_ref, bgu_ref, wd_ref, bd_ref, y_ref,
                   wgu_bf, wd_bf):
    del br_ref
    i = pl.program_id(0)
    d_exp = wd_ref.shape[1]
    used = i < nu_ref[0]
    new_expert = (i == 0) | (be_ref[i] != be_ref[jnp.maximum(i - 1, 0)])

    @pl.when(jnp.logical_not(used))
    def _():
        y_ref[...] = jnp.zeros_like(y_ref)

    @pl.when(used & new_expert)
    def _():
        _cast_rows(wgu_ref, wgu_bf, WEIGHT_CAST_ROWS)
        _cast_rows(wd_ref, wd_bf, WEIGHT_CAST_ROWS)

    @pl.when(used)
    def _():
        x = _load_token_tiles(x_ref, tm)
        gu = _dot(x.astype(BF16), wgu_bf[...]) + bgu_ref[0]
        glu = jnp.minimum(gu[:, :d_exp], SWIGLU_LIMIT)
        lin = jnp.clip(gu[:, d_exp:], -SWIGLU_LIMIT, SWIGLU_LIMIT)
        act = (lin + 1.0) * glu * jax.nn.sigmoid(SWIGLU_ALPHA * glu)
        _store_token_tiles(y_ref, _dot(act.astype(BF16), wd_bf[...]) + bd_ref[0])


def _experts(block_e, block_row, n_used, xs, n_blk, w_gu, b_gu, w_down, b_down, tm):
    n_exp, d_model, d_gu = w_gu.shape
    d_exp = w_down.shape[1]

    def w_map(i, be, br, nu):
        return (be[i], 0, 0)

    return pl.pallas_call(
        functools.partial(_expert_kernel, tm),
        grid_spec=pltpu.PrefetchScalarGridSpec(
            num_scalar_prefetch=3, grid=(n_blk,),
            in_specs=[pl.BlockSpec((tm * ROW_TILE, LANES), lambda i, be, br, nu: (br[i], 0)),
                      pl.BlockSpec((1, d_model, d_gu), w_map),
                      pl.BlockSpec((1, 1, d_gu), w_map),
                      pl.BlockSpec((1, d_exp, d_model), w_map),
                      pl.BlockSpec((1, 1, d_model), w_map)],
            out_specs=pl.BlockSpec((tm * ROW_TILE, LANES), lambda i, be, br, nu: (i, 0)),
            scratch_shapes=[pltpu.VMEM((d_model, d_gu), BF16),
                            pltpu.VMEM((d_exp, d_model), BF16)]),
        out_shape=jax.ShapeDtypeStruct((n_blk * tm * ROW_TILE, LANES), F32),
        compiler_params=pltpu.CompilerParams(
            dimension_semantics=("arbitrary",), vmem_limit_bytes=VMEM_LIMIT),
        name="experts",
    )(block_e, block_row, n_used, xs, w_gu, b_gu.reshape(n_exp, 1, d_gu), w_down,
      b_down.reshape(n_exp, 1, d_model))


def _combine_kernel(tt, n_tiles, dcur_ref, dnxt_ref, gate_ref, h1_ref, p_ref, ys_ref,
                    gple_ref, wgate_ref, wproj_ref, gfin_ref, y_ref, buf, sem):
    i = pl.program_id(0)
    slot = lax.rem(i, 2)

    def issue(dref, sl):
        def body(t, c):
            for k in range(TOP_K):
                pltpu.make_async_copy(_token_tile(ys_ref, dref[k * tt + t]),
                                      _token_tile(buf.at[sl, k], t), sem.at[sl]).start()
            return c
        lax.fori_loop(0, tt, body, 0, unroll=ISSUE_UNROLL)

    @pl.when(i == 0)
    def _():
        issue(dcur_ref, 0)

    @pl.when(i + 1 < n_tiles)
    def _():
        issue(dnxt_ref, 1 - slot)

    pltpu.make_async_copy(buf.at[slot], buf.at[slot], sem.at[slot]).wait()

    g = gate_ref[...]
    moe = g[:, 0:1] * _load_token_tiles(buf.at[slot, 0], tt)
    for k in range(1, TOP_K):
        moe = moe + g[:, k:k + 1] * _load_token_tiles(buf.at[slot, k], tt)
    h2 = h1_ref[...] + moe
    ple_gate = jax.nn.sigmoid(_dot(_rms(h2, gple_ref[...]).astype(BF16), wgate_ref[...]))
    proj = _dot(p_ref[...].astype(BF16), wproj_ref[...])
    h3 = h2 + proj * ple_gate
    y_ref[...] = _rms(h3, gfin_ref[...])


def _combine(dest_flat, gates_t, h1, p, ys, w, tt):
    tok, d_model = h1.shape
    d_ple = p.shape[-1]
    n_tiles = tok // tt

    def full(a):
        return pl.BlockSpec(a.shape, lambda i: (0,) * a.ndim)

    weights = [w["g_ple"], w["w_ple_gate"], w["w_ple_proj"], w["g_final"]]
    return pl.pallas_call(
        functools.partial(_combine_kernel, tt, n_tiles),
        grid=(n_tiles,),
        in_specs=[pl.BlockSpec((TOP_K * tt,), lambda i: (i,), memory_space=pltpu.SMEM),
                  pl.BlockSpec((TOP_K * tt,), lambda i: (jnp.minimum(i + 1, n_tiles - 1),),
                               memory_space=pltpu.SMEM),
                  pl.BlockSpec((tt, TOP_K), lambda i: (i, 0)),
                  pl.BlockSpec((tt, d_model), lambda i: (i, 0)),
                  pl.BlockSpec((tt, d_ple), lambda i: (i, 0)),
                  pl.BlockSpec(memory_space=pl.ANY)]
                 + [full(a) for a in weights],
        out_specs=pl.BlockSpec((tt, d_model), lambda i: (i, 0)),
        out_shape=jax.ShapeDtypeStruct((tok, d_model), F32),
        scratch_shapes=[pltpu.VMEM((2, TOP_K, tt * ROW_TILE, LANES), F32),
                        pltpu.SemaphoreType.DMA((2,))],
        compiler_params=pltpu.CompilerParams(
            dimension_semantics=("arbitrary",), vmem_limit_bytes=VMEM_LIMIT),
        name="combine",
    )(dest_flat, dest_flat, gates_t, h1, p, ys, *weights)


def _tile_major(a, tt):
    k, tok = a.shape
    return a.reshape(k, tok // tt, tt).transpose(1, 0, 2).reshape(-1)


def _lookup(table, idx):
    n = table.shape[0]
    hit = idx[..., None] == jnp.arange(n, dtype=idx.dtype)
    return jnp.sum(jnp.where(hit, table, 0), axis=-1)


def _pad_state(state, hist):
    bsz, n, ch = state.shape
    return jnp.concatenate([jnp.zeros((bsz, hist - n, ch), state.dtype), state], axis=1)


def _forward(x_prompt, x_sample, p_prompt, p_sample, state_conv, state_pool, lw, g_final,
             mix_rows, expert_rows, combine_rows):
    bp, sp, d_model = x_prompt.shape
    bs, ss, _ = x_sample.shape
    assert d_model == ROW_TILE * LANES, "token-tile layout needs a 1024-wide model"
    n_exp = lw["b_router"].shape[0]
    cch = lw["w_conv"].shape[-1]
    pch = lw["pool_scale"].shape[-1]
    tp, tsamp = bp * sp, bs * ss
    tm = expert_rows
    cap = -(-(tp + tsamp) // tm) * tm

    wr = lw["w_router"]
    wr_hi = wr.astype(BF16)
    wr_lo = (wr - wr_hi.astype(F32)).astype(BF16)
    wr_pad = jnp.concatenate(
        [wr_hi, wr_lo, jnp.zeros((d_model, LANES - 2 * n_exp), BF16)], axis=1)
    mw = dict(g_mix=lw["g_mix"].reshape(1, -1), w_in=lw["w_in"].astype(BF16),
              w_conv=lw["w_conv"], w_pool=lw["w_pool"].astype(BF16),
              pool_scale=lw["pool_scale"].reshape(1, -1), w_out=lw["w_out"].astype(BF16),
              g_ffn=lw["g_ffn"].reshape(1, -1), w_router=wr_pad,
              b_router=lw["b_router"].reshape(-1, 1))
    cw = dict(g_ple=lw["g_ple"].reshape(1, -1), w_ple_gate=lw["w_ple_gate"].astype(BF16),
              w_ple_proj=lw["w_ple_proj"].astype(BF16), g_final=g_final.reshape(1, -1))

    zc = jnp.zeros((bp, CONV_HIST, cch), F32)
    zp = jnp.zeros((bp, POOL_HIST, pch), F32)
    ts_p = min(mix_rows, sp)
    (h1_p, e_p, g_p, r_p, c_p, conv_p, pool_p, xs) = _mixer(
        x_prompt, zc, zp, jnp.zeros((n_exp, LANES), I32), None, n_exp * cap, cap, 0, mw, 1, ts_p)
    (h1_s, e_s, g_s, r_s, c_s, conv_s, pool_s, xs) = _mixer(
        x_sample, _pad_state(state_conv, CONV_HIST), _pad_state(state_pool, POOL_HIST),
        c_p, xs, n_exp * cap, cap, PAST_LEN, mw, bs, ss)

    counts = c_s[:, 0]
    nblk_e = (counts + tm - 1) // tm
    bend = jnp.cumsum(nblk_e)
    bstart = bend - nblk_e
    n_assign = TOP_K * (tp + tsamp)
    n_blk = -(-n_assign // tm) + n_exp
    n_used = bend[-1].astype(I32)
    blk = jnp.minimum(jnp.arange(n_blk, dtype=I32), n_used - 1)
    block_e = jnp.minimum(jnp.sum(bend[None, :] <= blk[:, None], axis=1), n_exp - 1).astype(I32)
    block_row = (block_e * (cap // tm) + blk - _lookup(bstart, block_e)).astype(I32)
    expert_ids = jnp.arange(n_exp, dtype=I32)
    xs = _zero_tails((expert_ids * cap + counts).astype(I32),
                     (nblk_e * tm - counts).astype(I32), xs)

    ys = _experts(block_e, block_row, n_used.reshape(1), xs, n_blk, lw["w_gu"], lw["b_gu"],
                  lw["w_down"], lw["b_down"], tm)

    dest_p = _lookup(bstart * tm, e_p) + r_p
    dest_s = _lookup(bstart * tm, e_s) + r_s
    tc_p = min(combine_rows, tp)
    y_p = _combine(_tile_major(dest_p, tc_p), g_p.T, h1_p.reshape(tp, d_model),
                   p_prompt.reshape(tp, -1), ys, cw, tc_p)
    y_s = _combine(_tile_major(dest_s, tsamp), g_s.T, h1_s.reshape(tsamp, d_model),
                   p_sample.reshape(tsamp, -1), ys, cw, tsamp)
    return (y_p.reshape(bp, sp, d_model), y_s.reshape(bs, ss, d_model),
            conv_p, pool_p, conv_s, pool_s)


def kernel(x_prompt, x_sample, p_prompt, p_sample, state_conv, state_pool, g_mix, w_in, w_conv,
           w_pool, pool_scale, w_out, g_ffn, w_router, b_router, w_gu, b_gu, w_down, b_down,
           g_ple, w_ple_proj, w_ple_gate, g_final):
    assert g_mix.shape[0] == 1, "single-layer step"
    lw = dict(g_mix=g_mix[0], w_in=w_in[0], w_conv=w_conv[0], w_pool=w_pool[0],
              pool_scale=pool_scale[0], w_out=w_out[0], g_ffn=g_ffn[0], w_router=w_router[0],
              b_router=b_router[0], w_gu=w_gu[0], b_gu=b_gu[0], w_down=w_down[0],
              b_down=b_down[0], g_ple=g_ple[0], w_ple_proj=w_ple_proj[0],
              w_ple_gate=w_ple_gate[0])
    y_p, y_s, conv_p, pool_p, conv_s, pool_s = _forward(
        x_prompt, x_sample, p_prompt[0], p_sample[0], state_conv[0], state_pool[0], lw, g_final,
        MIX_ROWS, EXPERT_ROWS, COMBINE_ROWS)
    return (y_p, y_s, conv_p[None], pool_p[None], conv_s[None], pool_s[None])
```

```python
import functools

import jax
import jax.numpy as jnp
from jax import lax
from jax.experimental import pallas as pl
from jax.experimental.pallas import tpu as pltpu

F32 = jnp.float32
BF16 = jnp.bfloat16
I32 = jnp.int32

PAST_LEN = 4096
POOL_WINDOWS = (2, 4, 8, 16)
TOP_K = 4
SWIGLU_LIMIT = 7.0
SWIGLU_ALPHA = 1.702
RMS_EPS = 1e-6

CONV_HIST = 8
POOL_HIST = 16
LANES = 128
ROW_TILE = 8
VMEM_LIMIT = 56 * 1024 * 1024

MIX_ROWS = 512
EXPERT_ROWS = 512
COMBINE_ROWS = 256
WEIGHT_CAST_ROWS = 128
ISSUE_UNROLL = 4
SCATTER_CHUNKS = 4


def _rms(x, g):
    return x * lax.rsqrt(jnp.mean(x * x, axis=-1, keepdims=True) + RMS_EPS) * g


def _dot(a, b):
    return jnp.dot(a, b, preferred_element_type=F32)


def _store_token_tiles(ref, x):
    n = x.shape[0]
    for j in range(ROW_TILE):
        ref[pl.ds(j, n, stride=ROW_TILE), :] = x[:, j * LANES:(j + 1) * LANES]


def _load_token_tiles(ref, n):
    return jnp.concatenate(
        [ref[pl.ds(j, n, stride=ROW_TILE), :] for j in range(ROW_TILE)], axis=1)


def _token_tile(ref, t):
    return ref.at[pl.ds(pl.multiple_of(t * ROW_TILE, ROW_TILE), ROW_TILE)]


MIXER_INPUTS = 14


def _mixer_kernel(start_pos, nb, ts, n_exp, cap, aliased, *refs):
    if aliased:
        refs = refs[:MIXER_INPUTS] + refs[MIXER_INPUTS + 1:]
    (x_ref, convp_ref, poolp_ref, cnt0_ref, gmix_ref, win_ref, wconv_ref, wpool_ref,
     pscale_ref, wout_ref, gffn_ref, wr_ref, br_ref, tri_ref,
     h1_ref, tope_ref, gate_ref, rank_ref, cnt_ref, convn_ref, pooln_ref, xs_ref,
     ev_ref, eu_ref, carry_ref, xf_st, dvm, dsm, sems, dsem) = refs
    b = pl.program_id(0)
    s = pl.program_id(1)
    step = b * pl.num_programs(1) + s
    n_steps = pl.num_programs(0) * pl.num_programs(1)
    slot = lax.rem(step, 2)
    rows = nb * ts
    d_model = x_ref.shape[-1]
    cch = wconv_ref.shape[-1]
    gch = wpool_ref.shape[-1]
    n_grp = wpool_ref.shape[0]

    @pl.when(step == 0)
    def _():
        carry_ref[...] = cnt0_ref[...].astype(F32)

    @pl.when(s == 0)
    def _():
        ev_ref[:, 0:CONV_HIST, :] = convp_ref[...]
        eu_ref[:, 0:POOL_HIST, :] = poolp_ref[...]

    def scatter(sl, lo, n):
        def issue(t, c):
            src = _token_tile(xf_st.at[sl], t)
            for k in range(TOP_K):
                pltpu.make_async_copy(src, _token_tile(xs_ref, dsm[k, t]), sems.at[sl]).start()
            return c
        lax.fori_loop(lo, lo + n, issue, 0, unroll=ISSUE_UNROLL)

    def scatter_prev_chunks(c, n):
        @pl.when(step > 0)
        def _():
            scatter(1 - slot, c * (rows // SCATTER_CHUNKS), n * (rows // SCATTER_CHUNKS))

    def wait_tile(sl):
        n = TOP_K * rows * ROW_TILE
        pltpu.make_async_copy(xs_ref.at[pl.ds(0, n)], xs_ref.at[pl.ds(0, n)], sems.at[sl]).wait()

    x = x_ref[...].reshape(rows, d_model)
    xn = _rms(x, gmix_ref[...])
    z = _dot(xn.astype(BF16), win_ref[...])
    scatter_prev_chunks(0, 1)
    bg = z[:, 0:cch]
    v = z[:, cch:2 * cch] * z[:, 2 * cch:3 * cch]
    u = z[:, 3 * cch:]

    w0 = wconv_ref[0:1, :]
    w1 = wconv_ref[1:2, :]
    w2 = wconv_ref[2:3, :]
    pos = start_pos + s * ts + lax.broadcasted_iota(I32, (ts, 1), 0)

    mix_rows = []
    for i in range(nb):
        r0 = i * ts
        v_i = v[r0:r0 + ts]
        u_i = u[r0:r0 + ts]
        ev_ref[i, CONV_HIST:CONV_HIST + ts, :] = v_i
        eu_ref[i, POOL_HIST:POOL_HIST + ts, :] = u_i
        conv = (w0 * ev_ref[i, CONV_HIST - 2:CONV_HIST - 2 + ts, :]
                + w1 * ev_ref[i, CONV_HIST - 1:CONV_HIST - 1 + ts, :]
                + w2 * v_i)
        y_a = bg[r0:r0 + ts] * conv

        e = eu_ref[i]
        sums = []
        cur = e
        shift = 1
        for g in range(n_grp):
            cur = cur + pltpu.roll(cur, shift, axis=0)
            sums.append(cur[POOL_HIST:POOL_HIST + ts, 0:gch])
            if g + 1 < n_grp:
                cur = cur[:, gch:]
            shift *= 2
        yb = []
        for g, w in enumerate(POOL_WINDOWS):
            cnt = jnp.minimum(pos + 1, w).astype(F32)
            diff = sums[g] / cnt - u_i[:, g * gch:(g + 1) * gch]
            yb.append(_dot(diff.astype(BF16), wpool_ref[g]))
        y_b = jnp.concatenate(yb, axis=1) * pscale_ref[...]
        mix_rows.append(jnp.concatenate([y_a, y_b], axis=1))

        convn_ref[i] = ev_ref[i, ts + CONV_HIST - 2:ts + CONV_HIST, :]
        pooln_ref[i] = eu_ref[i, ts + 1:ts + POOL_HIST, :]
        ev_ref[i, 0:CONV_HIST, :] = ev_ref[i, ts:ts + CONV_HIST, :]
        eu_ref[i, 0:POOL_HIST, :] = eu_ref[i, ts:ts + POOL_HIST, :]

    mix = mix_rows[0] if nb == 1 else jnp.concatenate(mix_rows, axis=0)
    h1 = x + _dot(mix.astype(BF16), wout_ref[...])
    h1_ref[...] = h1.reshape(nb, ts, d_model)
    xf = _rms(h1, gffn_ref[...])

    @pl.when(step > 1)
    def _():
        wait_tile(slot)
    _store_token_tiles(xf_st.at[slot], xf)
    scatter_prev_chunks(1, 1)

    lg = _dot(xf.astype(BF16), wr_ref[...])
    lgt = lg.T
    logits = lgt[0:n_exp] + lgt[n_exp:2 * n_exp] + br_ref[...]

    iota_e = lax.broadcasted_iota(I32, logits.shape, 0)
    top_l, top_e = [], []
    sel = jnp.zeros(logits.shape, F32)
    cur = logits
    for _ in range(TOP_K):
        m = jnp.max(cur, axis=0, keepdims=True)
        idx = jnp.min(jnp.where(cur == m, iota_e, n_exp), axis=0, keepdims=True)
        hit = iota_e == idx
        top_l.append(m)
        top_e.append(idx)
        sel = jnp.where(hit, 1.0, sel)
        cur = jnp.where(hit, -jnp.inf, cur)
    ex = [jnp.exp(l - top_l[0]) for l in top_l]
    den = ex[0] + ex[1] + ex[2] + ex[3]
    gate_ref[...] = jnp.concatenate([e_ / den for e_ in ex], axis=0)
    tope = jnp.concatenate(top_e, axis=0)
    tope_ref[...] = tope

    excl = _dot(sel.astype(BF16), tri_ref[...]) + carry_ref[:, 0:1]
    ranks = [jnp.sum(jnp.where(iota_e == e_, excl, 0.0), axis=0, keepdims=True) for e_ in top_e]
    rank = jnp.concatenate(ranks, axis=0).astype(I32)
    rank_ref[...] = rank
    carry = carry_ref[...] + jnp.sum(sel, axis=1, keepdims=True)
    carry_ref[...] = carry
    cnt_ref[...] = carry.astype(I32)

    scatter_prev_chunks(2, SCATTER_CHUNKS - 2)

    dvm[...] = tope * cap + rank
    to_smem = pltpu.make_async_copy(dvm, dsm, dsem)
    to_smem.start()
    to_smem.wait()

    @pl.when(step == n_steps - 1)
    def _():
        scatter(slot, 0, rows)

        @pl.when(step > 0)
        def _():
            wait_tile(1 - slot)
        wait_tile(slot)


def _mixer(x, conv_prev, pool_prev, cnt0, xs, xs_rows, cap, start_pos, w, nb, ts):
    bsz, seq, d_model = x.shape
    n_s = seq // ts
    n_b = bsz // nb
    rows = nb * ts
    tok = bsz * seq
    cch = w["w_conv"].shape[-1]
    pch = w["pool_scale"].shape[-1]
    n_exp = w["b_router"].shape[0]
    tri = (lax.broadcasted_iota(I32, (rows, rows), 0)
           < lax.broadcasted_iota(I32, (rows, rows), 1)).astype(BF16)

    def full(a):
        return pl.BlockSpec(a.shape, lambda b, s: (0,) * a.ndim)

    weights = [w["g_mix"], w["w_in"], w["w_conv"], w["w_pool"], w["pool_scale"], w["w_out"],
               w["g_ffn"], w["w_router"], w["b_router"], tri]
    tok_spec = pl.BlockSpec((TOP_K, rows), lambda b, s: (0, b * n_s + s))
    operands = [x, conv_prev, pool_prev, cnt0] + weights
    assert len(operands) == MIXER_INPUTS
    in_specs = ([pl.BlockSpec((nb, ts, d_model), lambda b, s: (b, s, 0)),
                 pl.BlockSpec((nb, CONV_HIST, cch), lambda b, s: (b, 0, 0)),
                 pl.BlockSpec((nb, POOL_HIST, pch), lambda b, s: (b, 0, 0)),
                 full(cnt0)] + [full(a) for a in weights])
    aliases = {}
    if xs is not None:
        operands.append(xs)
        in_specs.append(pl.BlockSpec(memory_space=pl.ANY))
        aliases = {MIXER_INPUTS: 7}
    return pl.pallas_call(
        functools.partial(_mixer_kernel, start_pos, nb, ts, n_exp, cap, xs is not None),
        grid=(n_b, n_s),
        in_specs=in_specs,
        out_specs=[pl.BlockSpec((nb, ts, d_model), lambda b, s: (b, s, 0)),
                   tok_spec, tok_spec, tok_spec,
                   pl.BlockSpec((n_exp, LANES), lambda b, s: (0, 0)),
                   pl.BlockSpec((nb, 2, cch), lambda b, s: (b, 0, 0)),
                   pl.BlockSpec((nb, POOL_HIST - 1, pch), lambda b, s: (b, 0, 0)),
                   pl.BlockSpec(memory_space=pl.ANY)],
        out_shape=[jax.ShapeDtypeStruct((bsz, seq, d_model), F32),
                   jax.ShapeDtypeStruct((TOP_K, tok), I32),
                   jax.ShapeDtypeStruct((TOP_K, tok), F32),
                   jax.ShapeDtypeStruct((TOP_K, tok), I32),
                   jax.ShapeDtypeStruct((n_exp, LANES), I32),
                   jax.ShapeDtypeStruct((bsz, 2, cch), F32),
                   jax.ShapeDtypeStruct((bsz, POOL_HIST - 1, pch), F32),
                   jax.ShapeDtypeStruct((xs_rows * ROW_TILE, LANES), F32)],
        scratch_shapes=[pltpu.VMEM((nb, ts + CONV_HIST, cch), F32),
                        pltpu.VMEM((nb, ts + POOL_HIST, pch), F32),
                        pltpu.VMEM((n_exp, LANES), F32),
                        pltpu.VMEM((2, rows * ROW_TILE, LANES), F32),
                        pltpu.VMEM((TOP_K, rows), I32),
                        pltpu.SMEM((TOP_K, rows), I32),
                        pltpu.SemaphoreType.DMA((2,)),
                        pltpu.SemaphoreType.DMA],
        input_output_aliases=aliases,
        compiler_params=pltpu.CompilerParams(
            dimension_semantics=("arbitrary", "arbitrary"), vmem_limit_bytes=VMEM_LIMIT),
        name="mixer",
    )(*operands)


def _tails_kernel(n_exp, n_bits, zs_ref, zn_ref, xs_in_ref, xs_ref, zero_ref, sem):
    del xs_in_ref
    zero_ref[...] = jnp.zeros_like(zero_ref)

    def each_piece(fn):
        def per_expert(e, c):
            zn = zn_ref[e]
            for p in reversed(range(n_bits)):
                size = 1 << p
                done = (zn >> (p + 1)) << (p + 1)

                @pl.when((zn & size) != 0)
                def _():
                    dst = pl.multiple_of((zs_ref[e] + done) * ROW_TILE, ROW_TILE)
                    fn(pltpu.make_async_copy(zero_ref.at[pl.ds(0, size * ROW_TILE)],
                                             xs_ref.at[pl.ds(dst, size * ROW_TILE)], sem))
            return c
        lax.fori_loop(0, n_exp, per_expert, 0)

    each_piece(lambda cp: cp.start())
    each_piece(lambda cp: cp.wait())


def _zero_tails(zstart, zcount, xs, max_count):
    n_bits = max_count.bit_length()
    return pl.pallas_call(
        functools.partial(_tails_kernel, zstart.shape[0], n_bits),
        grid_spec=pltpu.PrefetchScalarGridSpec(
            num_scalar_prefetch=2, grid=(1,),
            in_specs=[pl.BlockSpec(memory_space=pl.ANY)],
            out_specs=pl.BlockSpec(memory_space=pl.ANY),
            scratch_shapes=[pltpu.VMEM(((1 << (n_bits - 1)) * ROW_TILE, LANES), F32),
                            pltpu.SemaphoreType.DMA]),
        out_shape=jax.ShapeDtypeStruct(xs.shape, xs.dtype),
        input_output_aliases={2: 0},
        compiler_params=pltpu.CompilerParams(dimension_semantics=("arbitrary",)),
        name="zero_tails",
    )(zstart, zcount, xs)


def _cast_rows(src_ref, dst_ref, chunk):
    def body(c, carry):
        r = pl.multiple_of(c * chunk, chunk)
        dst_ref[pl.ds(r, chunk), :] = src_ref[0, pl.ds(r, chunk), :].astype(BF16)
        return carry
    lax.fori_loop(0, src_ref.shape[1] // chunk, body, 0)


def _expert_kernel(tm, be_ref, br_ref, nu_ref, x_ref, wgu_ref, bgu_ref, wd_ref, bd_ref, y_ref,
                   wgu_bf, wd_bf):
    del br_ref
    i = pl.program_id(0)
    d_exp = wd_ref.shape[1]
    used = i < nu_ref[0]
    new_expert = (i == 0) | (be_ref[i] != be_ref[jnp.maximum(i - 1, 0)])

    @pl.when(jnp.logical_not(used))
    def _():
        y_ref[...] = jnp.zeros_like(y_ref)

    @pl.when(used & new_expert)
    def _():
        _cast_rows(wgu_ref, wgu_bf, WEIGHT_CAST_ROWS)
        _cast_rows(wd_ref, wd_bf, WEIGHT_CAST_ROWS)

    @pl.when(used)
    def _():
        x = _load_token_tiles(x_ref, tm)
        gu = _dot(x.astype(BF16), wgu_bf[...]) + bgu_ref[0]
        glu = jnp.minimum(gu[:, :d_exp], SWIGLU_LIMIT)
        lin = jnp.clip(gu[:, d_exp:], -SWIGLU_LIMIT, SWIGLU_LIMIT)
        act = (lin + 1.0) * glu * jax.nn.sigmoid(SWIGLU_ALPHA * glu)
        _store_token_tiles(y_ref, _dot(act.astype(BF16), wd_bf[...]) + bd_ref[0])


def _experts(block_e, block_row, n_used, xs, n_blk, w_gu, b_gu, w_down, b_down, tm):
    n_exp, d_model, d_gu = w_gu.shape
    d_exp = w_down.shape[1]

    def w_map(i, be, br, nu):
        return (be[i], 0, 0)

    return pl.pallas_call(
        functools.partial(_expert_kernel, tm),
        grid_spec=pltpu.PrefetchScalarGridSpec(
            num_scalar_prefetch=3, grid=(n_blk,),
            in_specs=[pl.BlockSpec((tm * ROW_TILE, LANES), lambda i, be, br, nu: (br[i], 0)),
                      pl.BlockSpec((1, d_model, d_gu), w_map),
                      pl.BlockSpec((1, 1, d_gu), w_map),
                      pl.BlockSpec((1, d_exp, d_model), w_map),
                      pl.BlockSpec((1, 1, d_model), w_map)],
            out_specs=pl.BlockSpec((tm * ROW_TILE, LANES), lambda i, be, br, nu: (i, 0)),
            scratch_shapes=[pltpu.VMEM((d_model, d_gu), BF16),
                            pltpu.VMEM((d_exp, d_model), BF16)]),
        out_shape=jax.ShapeDtypeStruct((n_blk * tm * ROW_TILE, LANES), F32),
        compiler_params=pltpu.CompilerParams(
            dimension_semantics=("arbitrary",), vmem_limit_bytes=VMEM_LIMIT),
        name="experts",
    )(block_e, block_row, n_used, xs, w_gu, b_gu.reshape(n_exp, 1, d_gu), w_down,
      b_down.reshape(n_exp, 1, d_model))


def _combine_kernel(tt, n_tiles, dcur_ref, dnxt_ref, gate_ref, h1_ref, p_ref, ys_ref,
                    gple_ref, wgate_ref, wproj_ref, gfin_ref, y_ref, buf, sem):
    i = pl.program_id(0)
    slot = lax.rem(i, 2)

    def issue(dref, sl):
        def body(t, c):
            for k in range(TOP_K):
                pltpu.make_async_copy(_token_tile(ys_ref, dref[k * tt + t]),
                                      _token_tile(buf.at[sl, k], t), sem.at[sl]).start()
            return c
        lax.fori_loop(0, tt, body, 0, unroll=ISSUE_UNROLL)

    @pl.when(i == 0)
    def _():
        issue(dcur_ref, 0)

    @pl.when(i + 1 < n_tiles)
    def _():
        issue(dnxt_ref, 1 - slot)

    pltpu.make_async_copy(buf.at[slot], buf.at[slot], sem.at[slot]).wait()

    g = gate_ref[...]
    moe = g[:, 0:1] * _load_token_tiles(buf.at[slot, 0], tt)
    for k in range(1, TOP_K):
        moe = moe + g[:, k:k + 1] * _load_token_tiles(buf.at[slot, k], tt)
    h2 = h1_ref[...] + moe
    ple_gate = jax.nn.sigmoid(_dot(_rms(h2, gple_ref[...]).astype(BF16), wgate_ref[...]))
    proj = _dot(p_ref[...].astype(BF16), wproj_ref[...])
    h3 = h2 + proj * ple_gate
    y_ref[...] = _rms(h3, gfin_ref[...])


def _combine(dest_flat, gates_t, h1, p, ys, w, tt):
    tok, d_model = h1.shape
    d_ple = p.shape[-1]
    n_tiles = tok // tt

    def full(a):
        return pl.BlockSpec(a.shape, lambda i: (0,) * a.ndim)

    weights = [w["g_ple"], w["w_ple_gate"], w["w_ple_proj"], w["g_final"]]
    return pl.pallas_call(
        functools.partial(_combine_kernel, tt, n_tiles),
        grid=(n_tiles,),
        in_specs=[pl.BlockSpec((TOP_K * tt,), lambda i: (i,), memory_space=pltpu.SMEM),
                  pl.BlockSpec((TOP_K * tt,), lambda i: (jnp.minimum(i + 1, n_tiles - 1),),
                               memory_space=pltpu.SMEM),
                  pl.BlockSpec((tt, TOP_K), lambda i: (i, 0)),
                  pl.BlockSpec((tt, d_model), lambda i: (i, 0)),
                  pl.BlockSpec((tt, d_ple), lambda i: (i, 0)),
                  pl.BlockSpec(memory_space=pl.ANY)]
                 + [full(a) for a in weights],
        out_specs=pl.BlockSpec((tt, d_model), lambda i: (i, 0)),
        out_shape=jax.ShapeDtypeStruct((tok, d_model), F32),
        scratch_shapes=[pltpu.VMEM((2, TOP_K, tt * ROW_TILE, LANES), F32),
                        pltpu.SemaphoreType.DMA((2,))],
        compiler_params=pltpu.CompilerParams(
            dimension_semantics=("arbitrary",), vmem_limit_bytes=VMEM_LIMIT),
        name="combine",
    )(dest_flat, dest_flat, gates_t, h1, p, ys, *weights)


def _tile_major(a, tt):
    k, tok = a.shape
    return a.reshape(k, tok // tt, tt).transpose(1, 0, 2).reshape(-1)


def _lookup(table, idx):
    n = table.shape[0]
    hit = idx[..., None] == jnp.arange(n, dtype=idx.dtype)
    return jnp.sum(jnp.where(hit, table, 0), axis=-1)


def _pad_state(state, hist):
    bsz, n, ch = state.shape
    return jnp.concatenate([jnp.zeros((bsz, hist - n, ch), state.dtype), state], axis=1)


def _forward(x_prompt, x_sample, p_prompt, p_sample, state_conv, state_pool, lw, g_final,
             mix_rows, expert_rows, combine_rows):
    bp, sp, d_model = x_prompt.shape
    bs, ss, _ = x_sample.shape
    assert d_model == ROW_TILE * LANES, "token-tile layout needs a 1024-wide model"
    n_exp = lw["b_router"].shape[0]
    cch = lw["w_conv"].shape[-1]
    pch = lw["pool_scale"].shape[-1]
    tp, tsamp = bp * sp, bs * ss
    tm = expert_rows
    cap = -(-(tp + tsamp) // tm) * tm

    wr = lw["w_router"]
    wr_hi = wr.astype(BF16)
    wr_lo = (wr - wr_hi.astype(F32)).astype(BF16)
    wr_pad = jnp.concatenate(
        [wr_hi, wr_lo, jnp.zeros((d_model, LANES - 2 * n_exp), BF16)], axis=1)
    mw = dict(g_mix=lw["g_mix"].reshape(1, -1), w_in=lw["w_in"].astype(BF16),
              w_conv=lw["w_conv"], w_pool=lw["w_pool"].astype(BF16),
              pool_scale=lw["pool_scale"].reshape(1, -1), w_out=lw["w_out"].astype(BF16),
              g_ffn=lw["g_ffn"].reshape(1, -1), w_router=wr_pad,
              b_router=lw["b_router"].reshape(-1, 1))
    cw = dict(g_ple=lw["g_ple"].reshape(1, -1), w_ple_gate=lw["w_ple_gate"].astype(BF16),
              w_ple_proj=lw["w_ple_proj"].astype(BF16), g_final=g_final.reshape(1, -1))

    zc = jnp.zeros((bp, CONV_HIST, cch), F32)
    zp = jnp.zeros((bp, POOL_HIST, pch), F32)
    ts_p = min(mix_rows, sp)
    (h1_p, e_p, g_p, r_p, c_p, conv_p, pool_p, xs) = _mixer(
        x_prompt, zc, zp, jnp.zeros((n_exp, LANES), I32), None, n_exp * cap, cap, 0, mw, 1, ts_p)
    (h1_s, e_s, g_s, r_s, c_s, conv_s, pool_s, xs) = _mixer(
        x_sample, _pad_state(state_conv, CONV_HIST), _pad_state(state_pool, POOL_HIST),
        c_p, xs, n_exp * cap, cap, PAST_LEN, mw, bs, ss)

    counts = c_s[:, 0]
    nblk_e = (counts + tm - 1) // tm
    bend = jnp.cumsum(nblk_e)
    bstart = bend - nblk_e
    n_assign = TOP_K * (tp + tsamp)
    n_blk = -(-n_assign // tm) + n_exp
    n_used = bend[-1].astype(I32)
    blk = jnp.minimum(jnp.arange(n_blk, dtype=I32), jnp.maximum(n_used - 1, 0))
    block_e = jnp.minimum(jnp.sum(bend[None, :] <= blk[:, None], axis=1), n_exp - 1).astype(I32)
    block_row = (block_e * (cap // tm) + blk - _lookup(bstart, block_e)).astype(I32)
    expert_ids = jnp.arange(n_exp, dtype=I32)
    xs = _zero_tails((expert_ids * cap + counts).astype(I32),
                     (nblk_e * tm - counts).astype(I32), xs, tm - 1)

    ys = _experts(block_e, block_row, n_used.reshape(1), xs, n_blk, lw["w_gu"], lw["b_gu"],
                  lw["w_down"], lw["b_down"], tm)

    dest_p = _lookup(bstart * tm, e_p) + r_p
    dest_s = _lookup(bstart * tm, e_s) + r_s
    tc_p = min(combine_rows, tp)
    y_p = _combine(_tile_major(dest_p, tc_p), g_p.T, h1_p.reshape(tp, d_model),
                   p_prompt.reshape(tp, -1), ys, cw, tc_p)
    y_s = _combine(_tile_major(dest_s, tsamp), g_s.T, h1_s.reshape(tsamp, d_model),
                   p_sample.reshape(tsamp, -1), ys, cw, tsamp)
    return (y_p.reshape(bp, sp, d_model), y_s.reshape(bs, ss, d_model),
            conv_p, pool_p, conv_s, pool_s)


def kernel(x_prompt, x_sample, p_prompt, p_sample, state_conv, state_pool, g_mix, w_in, w_conv,
           w_pool, pool_scale, w_out, g_ffn, w_router, b_router, w_gu, b_gu, w_down, b_down,
           g_ple, w_ple_proj, w_ple_gate, g_final):
    assert g_mix.shape[0] == 1, "single-layer step"
    lw = dict(g_mix=g_mix[0], w_in=w_in[0], w_conv=w_conv[0], w_pool=w_pool[0],
              pool_scale=pool_scale[0], w_out=w_out[0], g_ffn=g_ffn[0], w_router=w_router[0],
              b_router=b_router[0], w_gu=w_gu[0], b_gu=b_gu[0], w_down=w_down[0],
              b_down=b_down[0], g_ple=g_ple[0], w_ple_proj=w_ple_proj[0],
              w_ple_gate=w_ple_gate[0])
    y_p, y_s, conv_p, pool_p, conv_s, pool_s = _forward(
        x_prompt, x_sample, p_prompt[0], p_sample[0], state_conv[0], state_pool[0], lw, g_final,
        MIX_ROWS, EXPERT_ROWS, COMBINE_ROWS)
    return (y_p, y_s, conv_p[None], pool_p[None], conv_s[None], pool_s[None])
```

```python
import functools

import jax
import jax.numpy as jnp
from jax import lax
from jax.experimental import pallas as pl
from jax.experimental.pallas import tpu as pltpu

F32 = jnp.float32
BF16 = jnp.bfloat16
I32 = jnp.int32

PAST_LEN = 4096
POOL_WINDOWS = (2, 4, 8, 16)
TOP_K = 4
SWIGLU_LIMIT = 7.0
SWIGLU_ALPHA = 1.702
RMS_EPS = 1e-6

CONV_HIST = 8
POOL_HIST = 16
LANES = 128
ROW_TILE = 8
VMEM_LIMIT = 56 * 1024 * 1024

MIX_ROWS = 512
EXPERT_ROWS = 512
COMBINE_ROWS = 256
COMBINE_GROUP = 64
WEIGHT_CAST_ROWS = 128
ISSUE_UNROLL = 4
SCATTER_CHUNKS = 4


def _rms(x, g):
    return x * lax.rsqrt(jnp.mean(x * x, axis=-1, keepdims=True) + RMS_EPS) * g


def _dot(a, b):
    return jnp.dot(a, b, preferred_element_type=F32)


def _store_token_tiles(ref, x):
    n = x.shape[0]
    for j in range(ROW_TILE):
        ref[pl.ds(j, n, stride=ROW_TILE), :] = x[:, j * LANES:(j + 1) * LANES]


def _load_token_tiles(ref, n):
    return jnp.concatenate(
        [ref[pl.ds(j, n, stride=ROW_TILE), :] for j in range(ROW_TILE)], axis=1)


def _token_tile(ref, t):
    return ref.at[pl.ds(pl.multiple_of(t * ROW_TILE, ROW_TILE), ROW_TILE)]


MIXER_INPUTS = 14


def _mixer_kernel(start_pos, nb, ts, n_exp, cap, aliased, *refs):
    if aliased:
        refs = refs[:MIXER_INPUTS] + refs[MIXER_INPUTS + 1:]
    (x_ref, convp_ref, poolp_ref, cnt0_ref, gmix_ref, win_ref, wconv_ref, wpool_ref,
     pscale_ref, wout_ref, gffn_ref, wr_ref, br_ref, tri_ref,
     h1_ref, tope_ref, gate_ref, rank_ref, cnt_ref, convn_ref, pooln_ref, xs_ref,
     ev_ref, eu_ref, carry_ref, xf_st, dvm, dsm, sems, dsem) = refs
    b = pl.program_id(0)
    s = pl.program_id(1)
    step = b * pl.num_programs(1) + s
    n_steps = pl.num_programs(0) * pl.num_programs(1)
    slot = lax.rem(step, 2)
    rows = nb * ts
    d_model = x_ref.shape[-1]
    cch = wconv_ref.shape[-1]
    gch = wpool_ref.shape[-1]
    n_grp = wpool_ref.shape[0]

    @pl.when(step == 0)
    def _():
        carry_ref[...] = cnt0_ref[...].astype(F32)

    @pl.when(s == 0)
    def _():
        ev_ref[:, 0:CONV_HIST, :] = convp_ref[...]
        eu_ref[:, 0:POOL_HIST, :] = poolp_ref[...]

    def scatter(sl, lo, n):
        def issue(t, c):
            src = _token_tile(xf_st.at[sl], t)
            for k in range(TOP_K):
                pltpu.make_async_copy(src, _token_tile(xs_ref, dsm[k, t]), sems.at[sl]).start()
            return c
        lax.fori_loop(lo, lo + n, issue, 0, unroll=ISSUE_UNROLL)

    def scatter_prev_chunks(c, n):
        @pl.when(step > 0)
        def _():
            scatter(1 - slot, c * (rows // SCATTER_CHUNKS), n * (rows // SCATTER_CHUNKS))

    def wait_tile(sl):
        n = TOP_K * rows * ROW_TILE
        pltpu.make_async_copy(xs_ref.at[pl.ds(0, n)], xs_ref.at[pl.ds(0, n)], sems.at[sl]).wait()

    x = x_ref[...].reshape(rows, d_model)
    xn = _rms(x, gmix_ref[...])
    z = _dot(xn.astype(BF16), win_ref[...])
    scatter_prev_chunks(0, 1)
    bg = z[:, 0:cch]
    v = z[:, cch:2 * cch] * z[:, 2 * cch:3 * cch]
    u = z[:, 3 * cch:]

    w0 = wconv_ref[0:1, :]
    w1 = wconv_ref[1:2, :]
    w2 = wconv_ref[2:3, :]
    pos = start_pos + s * ts + lax.broadcasted_iota(I32, (ts, 1), 0)

    mix_rows = []
    for i in range(nb):
        r0 = i * ts
        v_i = v[r0:r0 + ts]
        u_i = u[r0:r0 + ts]
        ev_ref[i, CONV_HIST:CONV_HIST + ts, :] = v_i
        eu_ref[i, POOL_HIST:POOL_HIST + ts, :] = u_i
        conv = (w0 * ev_ref[i, CONV_HIST - 2:CONV_HIST - 2 + ts, :]
                + w1 * ev_ref[i, CONV_HIST - 1:CONV_HIST - 1 + ts, :]
                + w2 * v_i)
        y_a = bg[r0:r0 + ts] * conv

        e = eu_ref[i]
        sums = []
        cur = e
        shift = 1
        for g in range(n_grp):
            cur = cur + pltpu.roll(cur, shift, axis=0)
            sums.append(cur[POOL_HIST:POOL_HIST + ts, 0:gch])
            if g + 1 < n_grp:
                cur = cur[:, gch:]
            shift *= 2
        yb = []
        for g, w in enumerate(POOL_WINDOWS):
            cnt = jnp.minimum(pos + 1, w).astype(F32)
            diff = sums[g] / cnt - u_i[:, g * gch:(g + 1) * gch]
            yb.append(_dot(diff.astype(BF16), wpool_ref[g]))
        y_b = jnp.concatenate(yb, axis=1) * pscale_ref[...]
        mix_rows.append(jnp.concatenate([y_a, y_b], axis=1))

        convn_ref[i] = ev_ref[i, ts + CONV_HIST - 2:ts + CONV_HIST, :]
        pooln_ref[i] = eu_ref[i, ts + 1:ts + POOL_HIST, :]
        ev_ref[i, 0:CONV_HIST, :] = ev_ref[i, ts:ts + CONV_HIST, :]
        eu_ref[i, 0:POOL_HIST, :] = eu_ref[i, ts:ts + POOL_HIST, :]

    mix = mix_rows[0] if nb == 1 else jnp.concatenate(mix_rows, axis=0)
    h1 = x + _dot(mix.astype(BF16), wout_ref[...])
    h1_ref[...] = h1.reshape(nb, ts, d_model)
    xf = _rms(h1, gffn_ref[...])

    @pl.when(step > 1)
    def _():
        wait_tile(slot)
    _store_token_tiles(xf_st.at[slot], xf)
    scatter_prev_chunks(1, 1)

    lg = _dot(xf.astype(BF16), wr_ref[...])
    lgt = lg.T
    logits = lgt[0:n_exp] + lgt[n_exp:2 * n_exp] + br_ref[...]

    iota_e = lax.broadcasted_iota(I32, logits.shape, 0)
    top_l, top_e = [], []
    sel = jnp.zeros(logits.shape, F32)
    cur = logits
    for _ in range(TOP_K):
        m = jnp.max(cur, axis=0, keepdims=True)
        idx = jnp.min(jnp.where(cur == m, iota_e, n_exp), axis=0, keepdims=True)
        hit = iota_e == idx
        top_l.append(m)
        top_e.append(idx)
        sel = jnp.where(hit, 1.0, sel)
        cur = jnp.where(hit, -jnp.inf, cur)
    ex = [jnp.exp(l - top_l[0]) for l in top_l]
    den = ex[0] + ex[1] + ex[2] + ex[3]
    gate_ref[...] = jnp.concatenate([e_ / den for e_ in ex], axis=0)
    tope = jnp.concatenate(top_e, axis=0)
    tope_ref[...] = tope

    excl = _dot(sel.astype(BF16), tri_ref[...]) + carry_ref[:, 0:1]
    ranks = [jnp.sum(jnp.where(iota_e == e_, excl, 0.0), axis=0, keepdims=True) for e_ in top_e]
    rank = jnp.concatenate(ranks, axis=0).astype(I32)
    rank_ref[...] = rank
    carry = carry_ref[...] + jnp.sum(sel, axis=1, keepdims=True)
    carry_ref[...] = carry
    cnt_ref[...] = carry.astype(I32)

    scatter_prev_chunks(2, SCATTER_CHUNKS - 2)

    dvm[...] = tope * cap + rank
    to_smem = pltpu.make_async_copy(dvm, dsm, dsem)
    to_smem.start()
    to_smem.wait()

    @pl.when(step == n_steps - 1)
    def _():
        scatter(slot, 0, rows)

        @pl.when(step > 0)
        def _():
            wait_tile(1 - slot)
        wait_tile(slot)


def _mixer(x, conv_prev, pool_prev, cnt0, xs, xs_rows, cap, start_pos, w, nb, ts):
    bsz, seq, d_model = x.shape
    n_s = seq // ts
    n_b = bsz // nb
    rows = nb * ts
    tok = bsz * seq
    cch = w["w_conv"].shape[-1]
    pch = w["pool_scale"].shape[-1]
    n_exp = w["b_router"].shape[0]
    tri = (lax.broadcasted_iota(I32, (rows, rows), 0)
           < lax.broadcasted_iota(I32, (rows, rows), 1)).astype(BF16)

    def full(a):
        return pl.BlockSpec(a.shape, lambda b, s: (0,) * a.ndim)

    weights = [w["g_mix"], w["w_in"], w["w_conv"], w["w_pool"], w["pool_scale"], w["w_out"],
               w["g_ffn"], w["w_router"], w["b_router"], tri]
    tok_spec = pl.BlockSpec((TOP_K, rows), lambda b, s: (0, b * n_s + s))
    operands = [x, conv_prev, pool_prev, cnt0] + weights
    assert len(operands) == MIXER_INPUTS
    in_specs = ([pl.BlockSpec((nb, ts, d_model), lambda b, s: (b, s, 0)),
                 pl.BlockSpec((nb, CONV_HIST, cch), lambda b, s: (b, 0, 0)),
                 pl.BlockSpec((nb, POOL_HIST, pch), lambda b, s: (b, 0, 0)),
                 full(cnt0)] + [full(a) for a in weights])
    aliases = {}
    if xs is not None:
        operands.append(xs)
        in_specs.append(pl.BlockSpec(memory_space=pl.ANY))
        aliases = {MIXER_INPUTS: 7}
    return pl.pallas_call(
        functools.partial(_mixer_kernel, start_pos, nb, ts, n_exp, cap, xs is not None),
        grid=(n_b, n_s),
        in_specs=in_specs,
        out_specs=[pl.BlockSpec((nb, ts, d_model), lambda b, s: (b, s, 0)),
                   tok_spec, tok_spec, tok_spec,
                   pl.BlockSpec((n_exp, LANES), lambda b, s: (0, 0)),
                   pl.BlockSpec((nb, 2, cch), lambda b, s: (b, 0, 0)),
                   pl.BlockSpec((nb, POOL_HIST - 1, pch), lambda b, s: (b, 0, 0)),
                   pl.BlockSpec(memory_space=pl.ANY)],
        out_shape=[jax.ShapeDtypeStruct((bsz, seq, d_model), F32),
                   jax.ShapeDtypeStruct((TOP_K, tok), I32),
                   jax.ShapeDtypeStruct((TOP_K, tok), F32),
                   jax.ShapeDtypeStruct((TOP_K, tok), I32),
                   jax.ShapeDtypeStruct((n_exp, LANES), I32),
                   jax.ShapeDtypeStruct((bsz, 2, cch), F32),
                   jax.ShapeDtypeStruct((bsz, POOL_HIST - 1, pch), F32),
                   jax.ShapeDtypeStruct((xs_rows * ROW_TILE, LANES), F32)],
        scratch_shapes=[pltpu.VMEM((nb, ts + CONV_HIST, cch), F32),
                        pltpu.VMEM((nb, ts + POOL_HIST, pch), F32),
                        pltpu.VMEM((n_exp, LANES), F32),
                        pltpu.VMEM((2, rows * ROW_TILE, LANES), F32),
                        pltpu.VMEM((TOP_K, rows), I32),
                        pltpu.SMEM((TOP_K, rows), I32),
                        pltpu.SemaphoreType.DMA((2,)),
                        pltpu.SemaphoreType.DMA],
        input_output_aliases=aliases,
        compiler_params=pltpu.CompilerParams(
            dimension_semantics=("arbitrary", "arbitrary"), vmem_limit_bytes=VMEM_LIMIT),
        name="mixer",
    )(*operands)


def _tails_kernel(n_exp, n_bits, zs_ref, zn_ref, xs_in_ref, xs_ref, zero_ref, sem):
    del xs_in_ref
    zero_ref[...] = jnp.zeros_like(zero_ref)

    def each_piece(fn):
        def per_expert(e, c):
            zn = zn_ref[e]
            for p in reversed(range(n_bits)):
                size = 1 << p
                done = (zn >> (p + 1)) << (p + 1)

                @pl.when((zn & size) != 0)
                def _():
                    dst = pl.multiple_of((zs_ref[e] + done) * ROW_TILE, ROW_TILE)
                    fn(pltpu.make_async_copy(zero_ref.at[pl.ds(0, size * ROW_TILE)],
                                             xs_ref.at[pl.ds(dst, size * ROW_TILE)], sem))
            return c
        lax.fori_loop(0, n_exp, per_expert, 0)

    each_piece(lambda cp: cp.start())
    each_piece(lambda cp: cp.wait())


def _zero_tails(zstart, zcount, xs, max_count):
    n_bits = max_count.bit_length()
    return pl.pallas_call(
        functools.partial(_tails_kernel, zstart.shape[0], n_bits),
        grid_spec=pltpu.PrefetchScalarGridSpec(
            num_scalar_prefetch=2, grid=(1,),
            in_specs=[pl.BlockSpec(memory_space=pl.ANY)],
            out_specs=pl.BlockSpec(memory_space=pl.ANY),
            scratch_shapes=[pltpu.VMEM(((1 << (n_bits - 1)) * ROW_TILE, LANES), F32),
                            pltpu.SemaphoreType.DMA]),
        out_shape=jax.ShapeDtypeStruct(xs.shape, xs.dtype),
        input_output_aliases={2: 0},
        compiler_params=pltpu.CompilerParams(dimension_semantics=("arbitrary",)),
        name="zero_tails",
    )(zstart, zcount, xs)


def _cast_rows(src_ref, dst_ref, chunk):
    def body(c, carry):
        r = pl.multiple_of(c * chunk, chunk)
        dst_ref[pl.ds(r, chunk), :] = src_ref[0, pl.ds(r, chunk), :].astype(BF16)
        return carry
    lax.fori_loop(0, src_ref.shape[1] // chunk, body, 0)


def _expert_kernel(tm, be_ref, br_ref, nu_ref, x_ref, wgu_ref, bgu_ref, wd_ref, bd_ref, y_ref,
                   wgu_bf, wd_bf):
    del br_ref
    i = pl.program_id(0)
    d_exp = wd_ref.shape[1]
    used = i < nu_ref[0]
    new_expert = (i == 0) | (be_ref[i] != be_ref[jnp.maximum(i - 1, 0)])

    @pl.when(jnp.logical_not(used))
    def _():
        y_ref[...] = jnp.zeros_like(y_ref)

    @pl.when(used & new_expert)
    def _():
        _cast_rows(wgu_ref, wgu_bf, WEIGHT_CAST_ROWS)
        _cast_rows(wd_ref, wd_bf, WEIGHT_CAST_ROWS)

    @pl.when(used)
    def _():
        x = _load_token_tiles(x_ref, tm)
        gu = _dot(x.astype(BF16), wgu_bf[...]) + bgu_ref[0]
        glu = jnp.minimum(gu[:, :d_exp], SWIGLU_LIMIT)
        lin = jnp.clip(gu[:, d_exp:], -SWIGLU_LIMIT, SWIGLU_LIMIT)
        act = (lin + 1.0) * glu * jax.nn.sigmoid(SWIGLU_ALPHA * glu)
        _store_token_tiles(y_ref, _dot(act.astype(BF16), wd_bf[...]) + bd_ref[0])


def _experts(block_e, block_row, n_used, xs, n_blk, w_gu, b_gu, w_down, b_down, tm):
    n_exp, d_model, d_gu = w_gu.shape
    d_exp = w_down.shape[1]

    def w_map(i, be, br, nu):
        return (be[i], 0, 0)

    return pl.pallas_call(
        functools.partial(_expert_kernel, tm),
        grid_spec=pltpu.PrefetchScalarGridSpec(
            num_scalar_prefetch=3, grid=(n_blk,),
            in_specs=[pl.BlockSpec((tm * ROW_TILE, LANES), lambda i, be, br, nu: (br[i], 0)),
                      pl.BlockSpec((1, d_model, d_gu), w_map),
                      pl.BlockSpec((1, 1, d_gu), w_map),
                      pl.BlockSpec((1, d_exp, d_model), w_map),
                      pl.BlockSpec((1, 1, d_model), w_map)],
            out_specs=pl.BlockSpec((tm * ROW_TILE, LANES), lambda i, be, br, nu: (i, 0)),
            scratch_shapes=[pltpu.VMEM((d_model, d_gu), BF16),
                            pltpu.VMEM((d_exp, d_model), BF16)]),
        out_shape=jax.ShapeDtypeStruct((n_blk * tm * ROW_TILE, LANES), F32),
        compiler_params=pltpu.CompilerParams(
            dimension_semantics=("arbitrary",), vmem_limit_bytes=VMEM_LIMIT),
        name="experts",
    )(block_e, block_row, n_used, xs, w_gu, b_gu.reshape(n_exp, 1, d_gu), w_down,
      b_down.reshape(n_exp, 1, d_model))


def _combine_kernel(tt, n_tiles, dcur_ref, dnxt_ref, gate_ref, h1_ref, p_ref, ys_ref,
                    gple_ref, wgate_ref, wproj_ref, gfin_ref, y_ref, buf, h2_st, xn_st, sem):
    i = pl.program_id(0)
    slot = lax.rem(i, 2)

    def issue(dref, sl):
        def body(t, c):
            for k in range(TOP_K):
                pltpu.make_async_copy(_token_tile(ys_ref, dref[k * tt + t]),
                                      _token_tile(buf.at[sl, k], t), sem.at[sl]).start()
            return c
        lax.fori_loop(0, tt, body, 0, unroll=ISSUE_UNROLL)

    def wait_tile(sl):
        pltpu.make_async_copy(buf.at[sl], buf.at[sl], sem.at[sl]).wait()

    @pl.when(i == 0)
    def _():
        issue(dcur_ref, 0)

    wait_tile(slot)

    def group(gi, c):
        r = pl.multiple_of(gi * COMBINE_GROUP, COMBINE_GROUP)
        g = gate_ref[pl.ds(r, COMBINE_GROUP), :]
        moe = None
        for k in range(TOP_K):
            rows_k = buf.at[slot, k, pl.ds(pl.multiple_of(r * ROW_TILE, ROW_TILE),
                                           COMBINE_GROUP * ROW_TILE)]
            term = g[:, k:k + 1] * _load_token_tiles(rows_k, COMBINE_GROUP)
            moe = term if moe is None else moe + term
        h2 = h1_ref[pl.ds(r, COMBINE_GROUP), :] + moe
        h2_st[pl.ds(r, COMBINE_GROUP), :] = h2
        xn_st[pl.ds(r, COMBINE_GROUP), :] = _rms(h2, gple_ref[...]).astype(BF16)
        for t in range(COMBINE_GROUP):
            for k in range(TOP_K):
                pltpu.make_async_copy(_token_tile(ys_ref, dnxt_ref[k * tt + r + t]),
                                      _token_tile(buf.at[1 - slot, k], r + t),
                                      sem.at[1 - slot]).start()
        return c
    lax.fori_loop(0, tt // COMBINE_GROUP, group, 0)

    ple_gate = jax.nn.sigmoid(_dot(xn_st[...], wgate_ref[...]))
    proj = _dot(p_ref[...].astype(BF16), wproj_ref[...])
    h3 = h2_st[...] + proj * ple_gate
    y_ref[...] = _rms(h3, gfin_ref[...])

    @pl.when(i == n_tiles - 1)
    def _():
        wait_tile(1 - slot)


def _combine(dest_flat, gates_t, h1, p, ys, w, tt):
    tok, d_model = h1.shape
    d_ple = p.shape[-1]
    n_tiles = tok // tt

    def full(a):
        return pl.BlockSpec(a.shape, lambda i: (0,) * a.ndim)

    weights = [w["g_ple"], w["w_ple_gate"], w["w_ple_proj"], w["g_final"]]
    return pl.pallas_call(
        functools.partial(_combine_kernel, tt, n_tiles),
        grid=(n_tiles,),
        in_specs=[pl.BlockSpec((TOP_K * tt,), lambda i: (i,), memory_space=pltpu.SMEM),
                  pl.BlockSpec((TOP_K * tt,), lambda i: (jnp.minimum(i + 1, n_tiles - 1),),
                               memory_space=pltpu.SMEM),
                  pl.BlockSpec((tt, TOP_K), lambda i: (i, 0)),
                  pl.BlockSpec((tt, d_model), lambda i: (i, 0)),
                  pl.BlockSpec((tt, d_ple), lambda i: (i, 0)),
                  pl.BlockSpec(memory_space=pl.ANY)]
                 + [full(a) for a in weights],
        out_specs=pl.BlockSpec((tt, d_model), lambda i: (i, 0)),
        out_shape=jax.ShapeDtypeStruct((tok, d_model), F32),
        scratch_shapes=[pltpu.VMEM((2, TOP_K, tt * ROW_TILE, LANES), F32),
                        pltpu.VMEM((tt, d_model), F32),
                        pltpu.VMEM((tt, d_model), BF16),
                        pltpu.SemaphoreType.DMA((2,))],
        compiler_params=pltpu.CompilerParams(
            dimension_semantics=("arbitrary",), vmem_limit_bytes=VMEM_LIMIT),
        name="combine",
    )(dest_flat, dest_flat, gates_t, h1, p, ys, *weights)


def _tile_major(a, tt):
    k, tok = a.shape
    return a.reshape(k, tok // tt, tt).transpose(1, 0, 2).reshape(-1)


def _lookup(table, idx):
    n = table.shape[0]
    hit = idx[..., None] == jnp.arange(n, dtype=idx.dtype)
    return jnp.sum(jnp.where(hit, table, 0), axis=-1)


def _pad_state(state, hist):
    bsz, n, ch = state.shape
    return jnp.concatenate([jnp.zeros((bsz, hist - n, ch), state.dtype), state], axis=1)


def _forward(x_prompt, x_sample, p_prompt, p_sample, state_conv, state_pool, lw, g_final,
             mix_rows, expert_rows, combine_rows):
    bp, sp, d_model = x_prompt.shape
    bs, ss, _ = x_sample.shape
    assert d_model == ROW_TILE * LANES, "token-tile layout needs a 1024-wide model"
    n_exp = lw["b_router"].shape[0]
    cch = lw["w_conv"].shape[-1]
    pch = lw["pool_scale"].shape[-1]
    tp, tsamp = bp * sp, bs * ss
    tm = expert_rows
    cap = -(-(tp + tsamp) // tm) * tm

    wr = lw["w_router"]
    wr_hi = wr.astype(BF16)
    wr_lo = (wr - wr_hi.astype(F32)).astype(BF16)
    wr_pad = jnp.concatenate(
        [wr_hi, wr_lo, jnp.zeros((d_model, LANES - 2 * n_exp), BF16)], axis=1)
    mw = dict(g_mix=lw["g_mix"].reshape(1, -1), w_in=lw["w_in"].astype(BF16),
              w_conv=lw["w_conv"], w_pool=lw["w_pool"].astype(BF16),
              pool_scale=lw["pool_scale"].reshape(1, -1), w_out=lw["w_out"].astype(BF16),
              g_ffn=lw["g_ffn"].reshape(1, -1), w_router=wr_pad,
              b_router=lw["b_router"].reshape(-1, 1))
    cw = dict(g_ple=lw["g_ple"].reshape(1, -1), w_ple_gate=lw["w_ple_gate"].astype(BF16),
              w_ple_proj=lw["w_ple_proj"].astype(BF16), g_final=g_final.reshape(1, -1))

    zc = jnp.zeros((bp, CONV_HIST, cch), F32)
    zp = jnp.zeros((bp, POOL_HIST, pch), F32)
    ts_p = min(mix_rows, sp)
    (h1_p, e_p, g_p, r_p, c_p, conv_p, pool_p, xs) = _mixer(
        x_prompt, zc, zp, jnp.zeros((n_exp, LANES), I32), None, n_exp * cap, cap, 0, mw, 1, ts_p)
    (h1_s, e_s, g_s, r_s, c_s, conv_s, pool_s, xs) = _mixer(
        x_sample, _pad_state(state_conv, CONV_HIST), _pad_state(state_pool, POOL_HIST),
        c_p, xs, n_exp * cap, cap, PAST_LEN, mw, bs, ss)

    counts = c_s[:, 0]
    nblk_e = (counts + tm - 1) // tm
    bend = jnp.cumsum(nblk_e)
    bstart = bend - nblk_e
    n_assign = TOP_K * (tp + tsamp)
    n_blk = -(-n_assign // tm) + n_exp
    n_used = bend[-1].astype(I32)
    blk = jnp.minimum(jnp.arange(n_blk, dtype=I32), jnp.maximum(n_used - 1, 0))
    block_e = jnp.minimum(jnp.sum(bend[None, :] <= blk[:, None], axis=1), n_exp - 1).astype(I32)
    block_row = (block_e * (cap // tm) + blk - _lookup(bstart, block_e)).astype(I32)
    expert_ids = jnp.arange(n_exp, dtype=I32)
    xs = _zero_tails((expert_ids * cap + counts).astype(I32),
                     (nblk_e * tm - counts).astype(I32), xs, tm - 1)

    ys = _experts(block_e, block_row, n_used.reshape(1), xs, n_blk, lw["w_gu"], lw["b_gu"],
                  lw["w_down"], lw["b_down"], tm)

    dest_p = _lookup(bstart * tm, e_p) + r_p
    dest_s = _lookup(bstart * tm, e_s) + r_s
    tc_p = min(combine_rows, tp)
    y_p = _combine(_tile_major(dest_p, tc_p), g_p.T, h1_p.reshape(tp, d_model),
                   p_prompt.reshape(tp, -1), ys, cw, tc_p)
    y_s = _combine(_tile_major(dest_s, tsamp), g_s.T, h1_s.reshape(tsamp, d_model),
                   p_sample.reshape(tsamp, -1), ys, cw, tsamp)
    return (y_p.reshape(bp, sp, d_model), y_s.reshape(bs, ss, d_model),
            conv_p, pool_p, conv_s, pool_s)


def kernel(x_prompt, x_sample, p_prompt, p_sample, state_conv, state_pool, g_mix, w_in, w_conv,
           w_pool, pool_scale, w_out, g_ffn, w_router, b_router, w_gu, b_gu, w_down, b_down,
           g_ple, w_ple_proj, w_ple_gate, g_final):
    assert g_mix.shape[0] == 1, "single-layer step"
    lw = dict(g_mix=g_mix[0], w_in=w_in[0], w_conv=w_conv[0], w_pool=w_pool[0],
              pool_scale=pool_scale[0], w_out=w_out[0], g_ffn=g_ffn[0], w_router=w_router[0],
              b_router=b_router[0], w_gu=w_gu[0], b_gu=b_gu[0], w_down=w_down[0],
              b_down=b_down[0], g_ple=g_ple[0], w_ple_proj=w_ple_proj[0],
              w_ple_gate=w_ple_gate[0])
    y_p, y_s, conv_p, pool_p, conv_s, pool_s = _forward(
        x_prompt, x_sample, p_prompt[0], p_sample[0], state_conv[0], state_pool[0], lw, g_final,
        MIX_ROWS, EXPERT_ROWS, COMBINE_ROWS)
    return (y_p, y_s, conv_p[None], pool_p[None], conv_s[None], pool_s[None])
```

```python
import functools

import jax
import jax.numpy as jnp
from jax import lax
from jax.experimental import pallas as pl
from jax.experimental.pallas import tpu as pltpu

F32 = jnp.float32
BF16 = jnp.bfloat16
I32 = jnp.int32

PAST_LEN = 4096
POOL_WINDOWS = (2, 4, 8, 16)
TOP_K = 4
SWIGLU_LIMIT = 7.0
SWIGLU_ALPHA = 1.702
RMS_EPS = 1e-6

CONV_HIST = 8
POOL_HIST = 16
LANES = 128
ROW_TILE = 8
VMEM_LIMIT = 56 * 1024 * 1024

MIX_ROWS = 512
EXPERT_ROWS = 512
COMBINE_ROWS = 256
WEIGHT_CAST_ROWS = 128
ISSUE_UNROLL = 8
SCATTER_CHUNKS = 4


def _rms(x, g):
    return x * lax.rsqrt(jnp.mean(x * x, axis=-1, keepdims=True) + RMS_EPS) * g


def _dot(a, b):
    return jnp.dot(a, b, preferred_element_type=F32)


def _store_token_tiles(ref, x):
    n = x.shape[0]
    for j in range(ROW_TILE):
        ref[pl.ds(j, n, stride=ROW_TILE), :] = x[:, j * LANES:(j + 1) * LANES]


def _load_token_tiles(ref, n):
    return jnp.concatenate(
        [ref[pl.ds(j, n, stride=ROW_TILE), :] for j in range(ROW_TILE)], axis=1)


def _token_tile(ref, t):
    return ref.at[pl.ds(pl.multiple_of(t * ROW_TILE, ROW_TILE), ROW_TILE)]


MIXER_INPUTS = 14


def _mixer_kernel(start_pos, nb, ts, n_exp, cap, aliased, *refs):
    if aliased:
        refs = refs[:MIXER_INPUTS] + refs[MIXER_INPUTS + 1:]
    (x_ref, convp_ref, poolp_ref, cnt0_ref, gmix_ref, win_ref, wconv_ref, wpool_ref,
     pscale_ref, wout_ref, gffn_ref, wr_ref, br_ref, tri_ref,
     h1_ref, tope_ref, gate_ref, rank_ref, cnt_ref, convn_ref, pooln_ref, xs_ref,
     ev_ref, eu_ref, carry_ref, xf_st, dvm, dsm, sems, dsem) = refs
    b = pl.program_id(0)
    s = pl.program_id(1)
    step = b * pl.num_programs(1) + s
    n_steps = pl.num_programs(0) * pl.num_programs(1)
    slot = lax.rem(step, 2)
    rows = nb * ts
    d_model = x_ref.shape[-1]
    cch = wconv_ref.shape[-1]
    gch = wpool_ref.shape[-1]
    n_grp = wpool_ref.shape[0]

    @pl.when(step == 0)
    def _():
        carry_ref[...] = cnt0_ref[...].astype(F32)

    @pl.when(s == 0)
    def _():
        ev_ref[:, 0:CONV_HIST, :] = convp_ref[...]
        eu_ref[:, 0:POOL_HIST, :] = poolp_ref[...]

    def scatter(sl, lo, n):
        def issue(t, c):
            src = _token_tile(xf_st.at[sl], t)
            for k in range(TOP_K):
                pltpu.make_async_copy(src, _token_tile(xs_ref, dsm[sl, k, t]),
                                      sems.at[sl]).start()
            return c
        lax.fori_loop(lo, lo + n, issue, 0, unroll=ISSUE_UNROLL)

    def scatter_prev_chunks(c, n):
        @pl.when(step > 0)
        def _():
            scatter(1 - slot, c * (rows // SCATTER_CHUNKS), n * (rows // SCATTER_CHUNKS))

    def wait_tile(sl):
        n = TOP_K * rows * ROW_TILE
        pltpu.make_async_copy(xs_ref.at[pl.ds(0, n)], xs_ref.at[pl.ds(0, n)], sems.at[sl]).wait()

    x = x_ref[...].reshape(rows, d_model)
    xn = _rms(x, gmix_ref[...])
    z = _dot(xn.astype(BF16), win_ref[...])
    scatter_prev_chunks(0, 1)
    bg = z[:, 0:cch]
    v = z[:, cch:2 * cch] * z[:, 2 * cch:3 * cch]
    u = z[:, 3 * cch:]

    w0 = wconv_ref[0:1, :]
    w1 = wconv_ref[1:2, :]
    w2 = wconv_ref[2:3, :]
    pos = start_pos + s * ts + lax.broadcasted_iota(I32, (ts, 1), 0)

    mix_rows = []
    for i in range(nb):
        r0 = i * ts
        v_i = v[r0:r0 + ts]
        u_i = u[r0:r0 + ts]
        ev_ref[i, CONV_HIST:CONV_HIST + ts, :] = v_i
        eu_ref[i, POOL_HIST:POOL_HIST + ts, :] = u_i
        conv = (w0 * ev_ref[i, CONV_HIST - 2:CONV_HIST - 2 + ts, :]
                + w1 * ev_ref[i, CONV_HIST - 1:CONV_HIST - 1 + ts, :]
                + w2 * v_i)
        y_a = bg[r0:r0 + ts] * conv

        e = eu_ref[i]
        sums = []
        cur = e
        shift = 1
        for g in range(n_grp):
            cur = cur + pltpu.roll(cur, shift, axis=0)
            sums.append(cur[POOL_HIST:POOL_HIST + ts, 0:gch])
            if g + 1 < n_grp:
                cur = cur[:, gch:]
            shift *= 2
        yb = []
        for g, w in enumerate(POOL_WINDOWS):
            cnt = jnp.minimum(pos + 1, w).astype(F32)
            diff = sums[g] / cnt - u_i[:, g * gch:(g + 1) * gch]
            yb.append(_dot(diff.astype(BF16), wpool_ref[g]))
        y_b = jnp.concatenate(yb, axis=1) * pscale_ref[...]
        mix_rows.append(jnp.concatenate([y_a, y_b], axis=1))

        convn_ref[i] = ev_ref[i, ts + CONV_HIST - 2:ts + CONV_HIST, :]
        pooln_ref[i] = eu_ref[i, ts + 1:ts + POOL_HIST, :]
        ev_ref[i, 0:CONV_HIST, :] = ev_ref[i, ts:ts + CONV_HIST, :]
        eu_ref[i, 0:POOL_HIST, :] = eu_ref[i, ts:ts + POOL_HIST, :]

    mix = mix_rows[0] if nb == 1 else jnp.concatenate(mix_rows, axis=0)
    h1 = x + _dot(mix.astype(BF16), wout_ref[...])
    h1_ref[...] = h1.reshape(nb, ts, d_model)
    xf = _rms(h1, gffn_ref[...])

    @pl.when(step > 1)
    def _():
        wait_tile(slot)
    _store_token_tiles(xf_st.at[slot], xf)
    scatter_prev_chunks(1, 1)

    lg = _dot(xf.astype(BF16), wr_ref[...])
    lgt = lg.T
    logits = lgt[0:n_exp] + lgt[n_exp:2 * n_exp] + br_ref[...]

    iota_e = lax.broadcasted_iota(I32, logits.shape, 0)
    top_l, top_e = [], []
    sel = jnp.zeros(logits.shape, F32)
    cur = logits
    for _ in range(TOP_K):
        m = jnp.max(cur, axis=0, keepdims=True)
        idx = jnp.min(jnp.where(cur == m, iota_e, n_exp), axis=0, keepdims=True)
        hit = iota_e == idx
        top_l.append(m)
        top_e.append(idx)
        sel = jnp.where(hit, 1.0, sel)
        cur = jnp.where(hit, -jnp.inf, cur)
    ex = [jnp.exp(l - top_l[0]) for l in top_l]
    den = ex[0] + ex[1] + ex[2] + ex[3]
    gate_ref[...] = jnp.concatenate([e_ / den for e_ in ex], axis=0)
    tope = jnp.concatenate(top_e, axis=0)
    tope_ref[...] = tope

    excl = _dot(sel.astype(BF16), tri_ref[...]) + carry_ref[:, 0:1]
    ranks = [jnp.sum(jnp.where(iota_e == e_, excl, 0.0), axis=0, keepdims=True) for e_ in top_e]
    rank = jnp.concatenate(ranks, axis=0).astype(I32)
    rank_ref[...] = rank
    carry = carry_ref[...] + jnp.sum(sel, axis=1, keepdims=True)
    carry_ref[...] = carry
    cnt_ref[...] = carry.astype(I32)

    dvm[...] = tope * cap + rank
    to_smem = pltpu.make_async_copy(dvm, dsm.at[slot], dsem)
    to_smem.start()
    scatter_prev_chunks(2, SCATTER_CHUNKS - 2)
    to_smem.wait()

    @pl.when(step == n_steps - 1)
    def _():
        scatter(slot, 0, rows)

        @pl.when(step > 0)
        def _():
            wait_tile(1 - slot)
        wait_tile(slot)


def _mixer(x, conv_prev, pool_prev, cnt0, xs, xs_rows, cap, start_pos, w, nb, ts):
    bsz, seq, d_model = x.shape
    n_s = seq // ts
    n_b = bsz // nb
    rows = nb * ts
    tok = bsz * seq
    cch = w["w_conv"].shape[-1]
    pch = w["pool_scale"].shape[-1]
    n_exp = w["b_router"].shape[0]
    tri = (lax.broadcasted_iota(I32, (rows, rows), 0)
           < lax.broadcasted_iota(I32, (rows, rows), 1)).astype(BF16)

    def full(a):
        return pl.BlockSpec(a.shape, lambda b, s: (0,) * a.ndim)

    weights = [w["g_mix"], w["w_in"], w["w_conv"], w["w_pool"], w["pool_scale"], w["w_out"],
               w["g_ffn"], w["w_router"], w["b_router"], tri]
    tok_spec = pl.BlockSpec((TOP_K, rows), lambda b, s: (0, b * n_s + s))
    operands = [x, conv_prev, pool_prev, cnt0] + weights
    assert len(operands) == MIXER_INPUTS
    in_specs = ([pl.BlockSpec((nb, ts, d_model), lambda b, s: (b, s, 0)),
                 pl.BlockSpec((nb, CONV_HIST, cch), lambda b, s: (b, 0, 0)),
                 pl.BlockSpec((nb, POOL_HIST, pch), lambda b, s: (b, 0, 0)),
                 full(cnt0)] + [full(a) for a in weights])
    aliases = {}
    if xs is not None:
        operands.append(xs)
        in_specs.append(pl.BlockSpec(memory_space=pl.ANY))
        aliases = {MIXER_INPUTS: 7}
    return pl.pallas_call(
        functools.partial(_mixer_kernel, start_pos, nb, ts, n_exp, cap, xs is not None),
        grid=(n_b, n_s),
        in_specs=in_specs,
        out_specs=[pl.BlockSpec((nb, ts, d_model), lambda b, s: (b, s, 0)),
                   tok_spec, tok_spec, tok_spec,
                   pl.BlockSpec((n_exp, LANES), lambda b, s: (0, 0)),
                   pl.BlockSpec((nb, 2, cch), lambda b, s: (b, 0, 0)),
                   pl.BlockSpec((nb, POOL_HIST - 1, pch), lambda b, s: (b, 0, 0)),
                   pl.BlockSpec(memory_space=pl.ANY)],
        out_shape=[jax.ShapeDtypeStruct((bsz, seq, d_model), F32),
                   jax.ShapeDtypeStruct((TOP_K, tok), I32),
                   jax.ShapeDtypeStruct((TOP_K, tok), F32),
                   jax.ShapeDtypeStruct((TOP_K, tok), I32),
                   jax.ShapeDtypeStruct((n_exp, LANES), I32),
                   jax.ShapeDtypeStruct((bsz, 2, cch), F32),
                   jax.ShapeDtypeStruct((bsz, POOL_HIST - 1, pch), F32),
                   jax.ShapeDtypeStruct((xs_rows * ROW_TILE, LANES), F32)],
        scratch_shapes=[pltpu.VMEM((nb, ts + CONV_HIST, cch), F32),
                        pltpu.VMEM((nb, ts + POOL_HIST, pch), F32),
                        pltpu.VMEM((n_exp, LANES), F32),
                        pltpu.VMEM((2, rows * ROW_TILE, LANES), F32),
                        pltpu.VMEM((TOP_K, rows), I32),
                        pltpu.SMEM((2, TOP_K, rows), I32),
                        pltpu.SemaphoreType.DMA((2,)),
                        pltpu.SemaphoreType.DMA],
        input_output_aliases=aliases,
        compiler_params=pltpu.CompilerParams(
            dimension_semantics=("arbitrary", "arbitrary"), vmem_limit_bytes=VMEM_LIMIT),
        name="mixer",
    )(*operands)


def _tails_kernel(n_exp, n_bits, zs_ref, zn_ref, xs_in_ref, xs_ref, zero_ref, sem):
    del xs_in_ref
    zero_ref[...] = jnp.zeros_like(zero_ref)

    def each_piece(fn):
        def per_expert(e, c):
            zn = zn_ref[e]
            for p in reversed(range(n_bits)):
                size = 1 << p
                done = (zn >> (p + 1)) << (p + 1)

                @pl.when((zn & size) != 0)
                def _():
                    dst = pl.multiple_of((zs_ref[e] + done) * ROW_TILE, ROW_TILE)
                    fn(pltpu.make_async_copy(zero_ref.at[pl.ds(0, size * ROW_TILE)],
                                             xs_ref.at[pl.ds(dst, size * ROW_TILE)], sem))
            return c
        lax.fori_loop(0, n_exp, per_expert, 0)

    each_piece(lambda cp: cp.start())
    each_piece(lambda cp: cp.wait())


def _zero_tails(zstart, zcount, xs, max_count):
    n_bits = max_count.bit_length()
    return pl.pallas_call(
        functools.partial(_tails_kernel, zstart.shape[0], n_bits),
        grid_spec=pltpu.PrefetchScalarGridSpec(
            num_scalar_prefetch=2, grid=(1,),
            in_specs=[pl.BlockSpec(memory_space=pl.ANY)],
            out_specs=pl.BlockSpec(memory_space=pl.ANY),
            scratch_shapes=[pltpu.VMEM(((1 << (n_bits - 1)) * ROW_TILE, LANES), F32),
                            pltpu.SemaphoreType.DMA]),
        out_shape=jax.ShapeDtypeStruct(xs.shape, xs.dtype),
        input_output_aliases={2: 0},
        compiler_params=pltpu.CompilerParams(dimension_semantics=("arbitrary",)),
        name="zero_tails",
    )(zstart, zcount, xs)


def _cast_rows(src_ref, dst_ref, chunk):
    def body(c, carry):
        r = pl.multiple_of(c * chunk, chunk)
        dst_ref[pl.ds(r, chunk), :] = src_ref[0, pl.ds(r, chunk), :].astype(BF16)
        return carry
    lax.fori_loop(0, src_ref.shape[1] // chunk, body, 0)


def _expert_kernel(tm, be_ref, br_ref, nu_ref, x_ref, wgu_ref, bgu_ref, wd_ref, bd_ref, y_ref,
                   wgu_bf, wd_bf):
    del br_ref
    i = pl.program_id(0)
    d_exp = wd_ref.shape[1]
    used = i < nu_ref[0]
    new_expert = (i == 0) | (be_ref[i] != be_ref[jnp.maximum(i - 1, 0)])

    @pl.when(jnp.logical_not(used))
    def _():
        y_ref[...] = jnp.zeros_like(y_ref)

    @pl.when(used & new_expert)
    def _():
        _cast_rows(wgu_ref, wgu_bf, WEIGHT_CAST_ROWS)
        _cast_rows(wd_ref, wd_bf, WEIGHT_CAST_ROWS)

    @pl.when(used)
    def _():
        x = _load_token_tiles(x_ref, tm)
        gu = _dot(x.astype(BF16), wgu_bf[...]) + bgu_ref[0]
        glu = jnp.minimum(gu[:, :d_exp], SWIGLU_LIMIT)
        lin = jnp.clip(gu[:, d_exp:], -SWIGLU_LIMIT, SWIGLU_LIMIT)
        act = (lin + 1.0) * glu * jax.nn.sigmoid(SWIGLU_ALPHA * glu)
        _store_token_tiles(y_ref, _dot(act.astype(BF16), wd_bf[...]) + bd_ref[0])


def _experts(block_e, block_row, n_used, xs, n_blk, w_gu, b_gu, w_down, b_down, tm):
    n_exp, d_model, d_gu = w_gu.shape
    d_exp = w_down.shape[1]

    def w_map(i, be, br, nu):
        return (be[i], 0, 0)

    return pl.pallas_call(
        functools.partial(_expert_kernel, tm),
        grid_spec=pltpu.PrefetchScalarGridSpec(
            num_scalar_prefetch=3, grid=(n_blk,),
            in_specs=[pl.BlockSpec((tm * ROW_TILE, LANES), lambda i, be, br, nu: (br[i], 0)),
                      pl.BlockSpec((1, d_model, d_gu), w_map),
                      pl.BlockSpec((1, 1, d_gu), w_map),
                      pl.BlockSpec((1, d_exp, d_model), w_map),
                      pl.BlockSpec((1, 1, d_model), w_map)],
            out_specs=pl.BlockSpec((tm * ROW_TILE, LANES), lambda i, be, br, nu: (i, 0)),
            scratch_shapes=[pltpu.VMEM((d_model, d_gu), BF16),
                            pltpu.VMEM((d_exp, d_model), BF16)]),
        out_shape=jax.ShapeDtypeStruct((n_blk * tm * ROW_TILE, LANES), F32),
        compiler_params=pltpu.CompilerParams(
            dimension_semantics=("arbitrary",), vmem_limit_bytes=VMEM_LIMIT),
        name="experts",
    )(block_e, block_row, n_used, xs, w_gu, b_gu.reshape(n_exp, 1, d_gu), w_down,
      b_down.reshape(n_exp, 1, d_model))


def _combine_kernel(tt, n_tiles, dcur_ref, dnxt_ref, gate_ref, h1_ref, p_ref, ys_ref,
                    gple_ref, wgate_ref, wproj_ref, gfin_ref, y_ref, buf, sem):
    i = pl.program_id(0)
    slot = lax.rem(i, 2)

    def issue(dref, sl):
        def body(t, c):
            for k in range(TOP_K):
                pltpu.make_async_copy(_token_tile(ys_ref, dref[k * tt + t]),
                                      _token_tile(buf.at[sl, k], t), sem.at[sl]).start()
            return c
        lax.fori_loop(0, tt, body, 0, unroll=ISSUE_UNROLL)

    @pl.when(i == 0)
    def _():
        issue(dcur_ref, 0)

    @pl.when(i + 1 < n_tiles)
    def _():
        issue(dnxt_ref, 1 - slot)

    pltpu.make_async_copy(buf.at[slot], buf.at[slot], sem.at[slot]).wait()

    g = gate_ref[...]
    moe = g[:, 0:1] * _load_token_tiles(buf.at[slot, 0], tt)
    for k in range(1, TOP_K):
        moe = moe + g[:, k:k + 1] * _load_token_tiles(buf.at[slot, k], tt)
    h2 = h1_ref[...] + moe
    ple_gate = jax.nn.sigmoid(_dot(_rms(h2, gple_ref[...]).astype(BF16), wgate_ref[...]))
    proj = _dot(p_ref[...].astype(BF16), wproj_ref[...])
    h3 = h2 + proj * ple_gate
    y_ref[...] = _rms(h3, gfin_ref[...])


def _combine(dest_flat, gates_t, h1, p, ys, w, tt):
    tok, d_model = h1.shape
    d_ple = p.shape[-1]
    n_tiles = tok // tt

    def full(a):
        return pl.BlockSpec(a.shape, lambda i: (0,) * a.ndim)

    weights = [w["g_ple"], w["w_ple_gate"], w["w_ple_proj"], w["g_final"]]
    return pl.pallas_call(
        functools.partial(_combine_kernel, tt, n_tiles),
        grid=(n_tiles,),
        in_specs=[pl.BlockSpec((TOP_K * tt,), lambda i: (i,), memory_space=pltpu.SMEM),
                  pl.BlockSpec((TOP_K * tt,), lambda i: (jnp.minimum(i + 1, n_tiles - 1),),
                               memory_space=pltpu.SMEM),
                  pl.BlockSpec((tt, TOP_K), lambda i: (i, 0)),
                  pl.BlockSpec((tt, d_model), lambda i: (i, 0)),
                  pl.BlockSpec((tt, d_ple), lambda i: (i, 0)),
                  pl.BlockSpec(memory_space=pl.ANY)]
                 + [full(a) for a in weights],
        out_specs=pl.BlockSpec((tt, d_model), lambda i: (i, 0)),
        out_shape=jax.ShapeDtypeStruct((tok, d_model), F32),
        scratch_shapes=[pltpu.VMEM((2, TOP_K, tt * ROW_TILE, LANES), F32),
                        pltpu.SemaphoreType.DMA((2,))],
        compiler_params=pltpu.CompilerParams(
            dimension_semantics=("arbitrary",), vmem_limit_bytes=VMEM_LIMIT),
        name="combine",
    )(dest_flat, dest_flat, gates_t, h1, p, ys, *weights)


def _tile_major(a, tt):
    k, tok = a.shape
    return a.reshape(k, tok // tt, tt).transpose(1, 0, 2).reshape(-1)


def _lookup(table, idx):
    n = table.shape[0]
    hit = idx[..., None] == jnp.arange(n, dtype=idx.dtype)
    return jnp.sum(jnp.where(hit, table, 0), axis=-1)


def _pad_state(state, hist):
    bsz, n, ch = state.shape
    return jnp.concatenate([jnp.zeros((bsz, hist - n, ch), state.dtype), state], axis=1)


def _forward(x_prompt, x_sample, p_prompt, p_sample, state_conv, state_pool, lw, g_final,
             mix_rows, expert_rows, combine_rows):
    bp, sp, d_model = x_prompt.shape
    bs, ss, _ = x_sample.shape
    assert d_model == ROW_TILE * LANES, "token-tile layout needs a 1024-wide model"
    n_exp = lw["b_router"].shape[0]
    cch = lw["w_conv"].shape[-1]
    pch = lw["pool_scale"].shape[-1]
    tp, tsamp = bp * sp, bs * ss
    tm = expert_rows
    cap = -(-(tp + tsamp) // tm) * tm

    wr = lw["w_router"]
    wr_hi = wr.astype(BF16)
    wr_lo = (wr - wr_hi.astype(F32)).astype(BF16)
    wr_pad = jnp.concatenate(
        [wr_hi, wr_lo, jnp.zeros((d_model, LANES - 2 * n_exp), BF16)], axis=1)
    mw = dict(g_mix=lw["g_mix"].reshape(1, -1), w_in=lw["w_in"].astype(BF16),
              w_conv=lw["w_conv"], w_pool=lw["w_pool"].astype(BF16),
              pool_scale=lw["pool_scale"].reshape(1, -1), w_out=lw["w_out"].astype(BF16),
              g_ffn=lw["g_ffn"].reshape(1, -1), w_router=wr_pad,
              b_router=lw["b_router"].reshape(-1, 1))
    cw = dict(g_ple=lw["g_ple"].reshape(1, -1), w_ple_gate=lw["w_ple_gate"].astype(BF16),
              w_ple_proj=lw["w_ple_proj"].astype(BF16), g_final=g_final.reshape(1, -1))

    zc = jnp.zeros((bp, CONV_HIST, cch), F32)
    zp = jnp.zeros((bp, POOL_HIST, pch), F32)
    ts_p = min(mix_rows, sp)
    (h1_p, e_p, g_p, r_p, c_p, conv_p, pool_p, xs) = _mixer(
        x_prompt, zc, zp, jnp.zeros((n_exp, LANES), I32), None, n_exp * cap, cap, 0, mw, 1, ts_p)
    (h1_s, e_s, g_s, r_s, c_s, conv_s, pool_s, xs) = _mixer(
        x_sample, _pad_state(state_conv, CONV_HIST), _pad_state(state_pool, POOL_HIST),
        c_p, xs, n_exp * cap, cap, PAST_LEN, mw, bs, ss)

    counts = c_s[:, 0]
    nblk_e = (counts + tm - 1) // tm
    bend = jnp.cumsum(nblk_e)
    bstart = bend - nblk_e
    n_assign = TOP_K * (tp + tsamp)
    n_blk = -(-n_assign // tm) + n_exp
    n_used = bend[-1].astype(I32)
    blk = jnp.minimum(jnp.arange(n_blk, dtype=I32), jnp.maximum(n_used - 1, 0))
    block_e = jnp.minimum(jnp.sum(bend[None, :] <= blk[:, None], axis=1), n_exp - 1).astype(I32)
    block_row = (block_e * (cap // tm) + blk - _lookup(bstart, block_e)).astype(I32)
    expert_ids = jnp.arange(n_exp, dtype=I32)
    xs = _zero_tails((expert_ids * cap + counts).astype(I32),
                     (nblk_e * tm - counts).astype(I32), xs, tm - 1)

    ys = _experts(block_e, block_row, n_used.reshape(1), xs, n_blk, lw["w_gu"], lw["b_gu"],
                  lw["w_down"], lw["b_down"], tm)

    dest_p = _lookup(bstart * tm, e_p) + r_p
    dest_s = _lookup(bstart * tm, e_s) + r_s
    tc_p = min(combine_rows, tp)
    y_p = _combine(_tile_major(dest_p, tc_p), g_p.T, h1_p.reshape(tp, d_model),
                   p_prompt.reshape(tp, -1), ys, cw, tc_p)
    y_s = _combine(_tile_major(dest_s, tsamp), g_s.T, h1_s.reshape(tsamp, d_model),
                   p_sample.reshape(tsamp, -1), ys, cw, tsamp)
    return (y_p.reshape(bp, sp, d_model), y_s.reshape(bs, ss, d_model),
            conv_p, pool_p, conv_s, pool_s)


def kernel(x_prompt, x_sample, p_prompt, p_sample, state_conv, state_pool, g_mix, w_in, w_conv,
           w_pool, pool_scale, w_out, g_ffn, w_router, b_router, w_gu, b_gu, w_down, b_down,
           g_ple, w_ple_proj, w_ple_gate, g_final):
    assert g_mix.shape[0] == 1, "single-layer step"
    lw = dict(g_mix=g_mix[0], w_in=w_in[0], w_conv=w_conv[0], w_pool=w_pool[0],
              pool_scale=pool_scale[0], w_out=w_out[0], g_ffn=g_ffn[0], w_router=w_router[0],
              b_router=b_router[0], w_gu=w_gu[0], b_gu=b_gu[0], w_down=w_down[0],
              b_down=b_down[0], g_ple=g_ple[0], w_ple_proj=w_ple_proj[0],
              w_ple_gate=w_ple_gate[0])
    y_p, y_s, conv_p, pool_p, conv_s, pool_s = _forward(
        x_prompt, x_sample, p_prompt[0], p_sample[0], state_conv[0], state_pool[0], lw, g_final,
        MIX_ROWS, EXPERT_ROWS, COMBINE_ROWS)
    return (y_p, y_s, conv_p[None], pool_p[None], conv_s[None], pool_s[None])
```

```python
import functools

import jax
import jax.numpy as jnp
from jax import lax
from jax.experimental import pallas as pl
from jax.experimental.pallas import tpu as pltpu

F32 = jnp.float32
BF16 = jnp.bfloat16
I32 = jnp.int32

PAST_LEN = 4096
POOL_WINDOWS = (2, 4, 8, 16)
TOP_K = 4
SWIGLU_LIMIT = 7.0
SWIGLU_ALPHA = 1.702
RMS_EPS = 1e-6

CONV_HIST = 8
POOL_HIST = 16
LANES = 128
ROW_TILE = 8
VMEM_LIMIT = 56 * 1024 * 1024

MIX_ROWS = 512
EXPERT_ROWS = 512
COMBINE_ROWS = 256
WEIGHT_CAST_ROWS = 128
ISSUE_UNROLL = 8
SCATTER_CHUNKS = 4


def _rms(x, g):
    return x * lax.rsqrt(jnp.mean(x * x, axis=-1, keepdims=True) + RMS_EPS) * g


def _dot(a, b):
    return jnp.dot(a, b, preferred_element_type=F32)


def _store_token_tiles(ref, x):
    n = x.shape[0]
    for j in range(ROW_TILE):
        ref[pl.ds(j, n, stride=ROW_TILE), :] = x[:, j * LANES:(j + 1) * LANES]


def _load_token_tiles(ref, n):
    return jnp.concatenate(
        [ref[pl.ds(j, n, stride=ROW_TILE), :] for j in range(ROW_TILE)], axis=1)


def _token_tile(ref, t):
    return ref.at[pl.ds(pl.multiple_of(t * ROW_TILE, ROW_TILE), ROW_TILE)]


MIXER_INPUTS = 14


def _mixer_kernel(start_pos, nb, ts, n_exp, cap, aliased, *refs):
    if aliased:
        refs = refs[:MIXER_INPUTS] + refs[MIXER_INPUTS + 1:]
    (x_ref, convp_ref, poolp_ref, cnt0_ref, gmix_ref, win_ref, wconv_ref, wpool_ref,
     pscale_ref, wout_ref, gffn_ref, wr_ref, br_ref, tri_ref,
     h1_ref, tope_ref, gate_ref, rank_ref, cnt_ref, convn_ref, pooln_ref, xs_ref,
     ev_ref, eu_ref, carry_ref, xf_st, dvm, dsm, sems, dsem) = refs
    b = pl.program_id(0)
    s = pl.program_id(1)
    step = b * pl.num_programs(1) + s
    n_steps = pl.num_programs(0) * pl.num_programs(1)
    slot = lax.rem(step, 2)
    rows = nb * ts
    d_model = x_ref.shape[-1]
    cch = wconv_ref.shape[-1]
    gch = wpool_ref.shape[-1]
    n_grp = wpool_ref.shape[0]

    @pl.when(step == 0)
    def _():
        carry_ref[...] = cnt0_ref[...].astype(F32)

    @pl.when(s == 0)
    def _():
        ev_ref[:, 0:CONV_HIST, :] = convp_ref[...]
        eu_ref[:, 0:POOL_HIST, :] = poolp_ref[...]

    def scatter(sl, lo, n):
        def issue(t, c):
            src = _token_tile(xf_st.at[sl], t)
            for k in range(TOP_K):
                pltpu.make_async_copy(src, _token_tile(xs_ref, dsm[sl, k, t]),
                                      sems.at[sl]).start()
            return c
        lax.fori_loop(lo, lo + n, issue, 0, unroll=ISSUE_UNROLL)

    def scatter_prev_chunks(c, n):
        @pl.when(step > 0)
        def _():
            scatter(1 - slot, c * (rows // SCATTER_CHUNKS), n * (rows // SCATTER_CHUNKS))

    def wait_tile(sl):
        n = TOP_K * rows * ROW_TILE
        pltpu.make_async_copy(xs_ref.at[pl.ds(0, n)], xs_ref.at[pl.ds(0, n)], sems.at[sl]).wait()

    x = x_ref[...].reshape(rows, d_model)
    xn = _rms(x, gmix_ref[...])
    z = _dot(xn.astype(BF16), win_ref[...])
    scatter_prev_chunks(0, 1)
    bg = z[:, 0:cch]
    v = z[:, cch:2 * cch] * z[:, 2 * cch:3 * cch]
    u = z[:, 3 * cch:]

    w0 = wconv_ref[0:1, :]
    w1 = wconv_ref[1:2, :]
    w2 = wconv_ref[2:3, :]
    pos = start_pos + s * ts + lax.broadcasted_iota(I32, (ts, 1), 0)

    mix_rows = []
    for i in range(nb):
        r0 = i * ts
        v_i = v[r0:r0 + ts]
        u_i = u[r0:r0 + ts]
        ev_ref[i, CONV_HIST:CONV_HIST + ts, :] = v_i
        eu_ref[i, POOL_HIST:POOL_HIST + ts, :] = u_i
        conv = (w0 * ev_ref[i, CONV_HIST - 2:CONV_HIST - 2 + ts, :]
                + w1 * ev_ref[i, CONV_HIST - 1:CONV_HIST - 1 + ts, :]
                + w2 * v_i)
        y_a = bg[r0:r0 + ts] * conv

        e = eu_ref[i]
        sums = []
        cur = e
        shift = 1
        for g in range(n_grp):
            cur = cur + pltpu.roll(cur, shift, axis=0)
            sums.append(cur[POOL_HIST:POOL_HIST + ts, 0:gch])
            if g + 1 < n_grp:
                cur = cur[:, gch:]
            shift *= 2
        yb = []
        for g, w in enumerate(POOL_WINDOWS):
            cnt = jnp.minimum(pos + 1, w).astype(F32)
            diff = sums[g] / cnt - u_i[:, g * gch:(g + 1) * gch]
            yb.append(_dot(diff.astype(BF16), wpool_ref[g]))
        y_b = jnp.concatenate(yb, axis=1) * pscale_ref[...]
        mix_rows.append(jnp.concatenate([y_a, y_b], axis=1))

        convn_ref[i] = ev_ref[i, ts + CONV_HIST - 2:ts + CONV_HIST, :]
        pooln_ref[i] = eu_ref[i, ts + 1:ts + POOL_HIST, :]
        ev_ref[i, 0:CONV_HIST, :] = ev_ref[i, ts:ts + CONV_HIST, :]
        eu_ref[i, 0:POOL_HIST, :] = eu_ref[i, ts:ts + POOL_HIST, :]

    mix = mix_rows[0] if nb == 1 else jnp.concatenate(mix_rows, axis=0)
    h1 = x + _dot(mix.astype(BF16), wout_ref[...])
    h1_ref[...] = h1.reshape(nb, ts, d_model)
    xf = _rms(h1, gffn_ref[...])

    @pl.when(step > 1)
    def _():
        wait_tile(slot)
    _store_token_tiles(xf_st.at[slot], xf)
    scatter_prev_chunks(1, 1)

    lg = _dot(xf.astype(BF16), wr_ref[...])
    lgt = lg.T
    logits = lgt[0:n_exp] + lgt[n_exp:2 * n_exp] + br_ref[...]

    iota_e = lax.broadcasted_iota(I32, logits.shape, 0)
    top_l, top_e = [], []
    sel = jnp.zeros(logits.shape, F32)
    cur = logits
    for _ in range(TOP_K):
        m = jnp.max(cur, axis=0, keepdims=True)
        idx = jnp.min(jnp.where(cur == m, iota_e, n_exp), axis=0, keepdims=True)
        hit = iota_e == idx
        top_l.append(m)
        top_e.append(idx)
        sel = jnp.where(hit, 1.0, sel)
        cur = jnp.where(hit, -jnp.inf, cur)
    ex = [jnp.exp(l - top_l[0]) for l in top_l]
    den = ex[0] + ex[1] + ex[2] + ex[3]
    gate_ref[...] = jnp.concatenate([e_ / den for e_ in ex], axis=0)
    tope = jnp.concatenate(top_e, axis=0)
    tope_ref[...] = tope

    excl = _dot(sel.astype(BF16), tri_ref[...]) + carry_ref[:, 0:1]
    ranks = [jnp.sum(jnp.where(iota_e == e_, excl, 0.0), axis=0, keepdims=True) for e_ in top_e]
    rank = jnp.concatenate(ranks, axis=0).astype(I32)
    rank_ref[...] = rank
    carry = carry_ref[...] + jnp.sum(sel, axis=1, keepdims=True)
    carry_ref[...] = carry
    cnt_ref[...] = carry.astype(I32)

    dvm[...] = tope * cap + rank
    to_smem = pltpu.make_async_copy(dvm, dsm.at[slot], dsem)
    to_smem.start()
    scatter_prev_chunks(2, SCATTER_CHUNKS - 2)
    to_smem.wait()

    @pl.when(step == n_steps - 1)
    def _():
        scatter(slot, 0, rows)

        @pl.when(step > 0)
        def _():
            wait_tile(1 - slot)
        wait_tile(slot)


def _mixer(x, conv_prev, pool_prev, cnt0, xs, xs_rows, cap, start_pos, w, nb, ts):
    bsz, seq, d_model = x.shape
    n_s = seq // ts
    n_b = bsz // nb
    rows = nb * ts
    tok = bsz * seq
    cch = w["w_conv"].shape[-1]
    pch = w["pool_scale"].shape[-1]
    n_exp = w["b_router"].shape[0]
    tri = (lax.broadcasted_iota(I32, (rows, rows), 0)
           < lax.broadcasted_iota(I32, (rows, rows), 1)).astype(BF16)

    def full(a):
        return pl.BlockSpec(a.shape, lambda b, s: (0,) * a.ndim)

    weights = [w["g_mix"], w["w_in"], w["w_conv"], w["w_pool"], w["pool_scale"], w["w_out"],
               w["g_ffn"], w["w_router"], w["b_router"], tri]
    tok_spec = pl.BlockSpec((TOP_K, rows), lambda b, s: (0, b * n_s + s))
    operands = [x, conv_prev, pool_prev, cnt0] + weights
    assert len(operands) == MIXER_INPUTS
    in_specs = ([pl.BlockSpec((nb, ts, d_model), lambda b, s: (b, s, 0)),
                 pl.BlockSpec((nb, CONV_HIST, cch), lambda b, s: (b, 0, 0)),
                 pl.BlockSpec((nb, POOL_HIST, pch), lambda b, s: (b, 0, 0)),
                 full(cnt0)] + [full(a) for a in weights])
    aliases = {}
    if xs is not None:
        operands.append(xs)
        in_specs.append(pl.BlockSpec(memory_space=pl.ANY))
        aliases = {MIXER_INPUTS: 7}
    return pl.pallas_call(
        functools.partial(_mixer_kernel, start_pos, nb, ts, n_exp, cap, xs is not None),
        grid=(n_b, n_s),
        in_specs=in_specs,
        out_specs=[pl.BlockSpec((nb, ts, d_model), lambda b, s: (b, s, 0)),
                   tok_spec, tok_spec, tok_spec,
                   pl.BlockSpec((n_exp, LANES), lambda b, s: (0, 0)),
                   pl.BlockSpec((nb, 2, cch), lambda b, s: (b, 0, 0)),
                   pl.BlockSpec((nb, POOL_HIST - 1, pch), lambda b, s: (b, 0, 0)),
                   pl.BlockSpec(memory_space=pl.ANY)],
        out_shape=[jax.ShapeDtypeStruct((bsz, seq, d_model), F32),
                   jax.ShapeDtypeStruct((TOP_K, tok), I32),
                   jax.ShapeDtypeStruct((TOP_K, tok), F32),
                   jax.ShapeDtypeStruct((TOP_K, tok), I32),
                   jax.ShapeDtypeStruct((n_exp, LANES), I32),
                   jax.ShapeDtypeStruct((bsz, 2, cch), F32),
                   jax.ShapeDtypeStruct((bsz, POOL_HIST - 1, pch), F32),
                   jax.ShapeDtypeStruct((xs_rows * ROW_TILE, LANES), F32)],
        scratch_shapes=[pltpu.VMEM((nb, ts + CONV_HIST, cch), F32),
                        pltpu.VMEM((nb, ts + POOL_HIST, pch), F32),
                        pltpu.VMEM((n_exp, LANES), F32),
                        pltpu.VMEM((2, rows * ROW_TILE, LANES), F32),
                        pltpu.VMEM((TOP_K, rows), I32),
                        pltpu.SMEM((2, TOP_K, rows), I32),
                        pltpu.SemaphoreType.DMA((2,)),
                        pltpu.SemaphoreType.DMA],
        input_output_aliases=aliases,
        compiler_params=pltpu.CompilerParams(
            dimension_semantics=("arbitrary", "arbitrary"), vmem_limit_bytes=VMEM_LIMIT),
        name="mixer",
    )(*operands)


def _tails_kernel(n_exp, n_bits, zs_ref, zn_ref, xs_in_ref, xs_ref, zero_ref, sem):
    del xs_in_ref
    zero_ref[...] = jnp.zeros_like(zero_ref)

    def each_piece(fn):
        def per_expert(e, c):
            zn = zn_ref[e]
            for p in reversed(range(n_bits)):
                size = 1 << p
                done = (zn >> (p + 1)) << (p + 1)

                @pl.when((zn & size) != 0)
                def _():
                    dst = pl.multiple_of((zs_ref[e] + done) * ROW_TILE, ROW_TILE)
                    fn(pltpu.make_async_copy(zero_ref.at[pl.ds(0, size * ROW_TILE)],
                                             xs_ref.at[pl.ds(dst, size * ROW_TILE)], sem))
            return c
        lax.fori_loop(0, n_exp, per_expert, 0)

    each_piece(lambda cp: cp.start())
    each_piece(lambda cp: cp.wait())


def _zero_tails(zstart, zcount, xs, max_count):
    n_bits = max_count.bit_length()
    return pl.pallas_call(
        functools.partial(_tails_kernel, zstart.shape[0], n_bits),
        grid_spec=pltpu.PrefetchScalarGridSpec(
            num_scalar_prefetch=2, grid=(1,),
            in_specs=[pl.BlockSpec(memory_space=pl.ANY)],
            out_specs=pl.BlockSpec(memory_space=pl.ANY),
            scratch_shapes=[pltpu.VMEM(((1 << (n_bits - 1)) * ROW_TILE, LANES), F32),
                            pltpu.SemaphoreType.DMA]),
        out_shape=jax.ShapeDtypeStruct(xs.shape, xs.dtype),
        input_output_aliases={2: 0},
        compiler_params=pltpu.CompilerParams(dimension_semantics=("arbitrary",)),
        name="zero_tails",
    )(zstart, zcount, xs)


def _cast_rows(src_ref, dst_ref, chunk):
    def body(c, carry):
        r = pl.multiple_of(c * chunk, chunk)
        dst_ref[pl.ds(r, chunk), :] = src_ref[pl.ds(r, chunk), :].astype(BF16)
        return carry
    lax.fori_loop(0, src_ref.shape[0] // chunk, body, 0)


def _expert_kernel(tm, be_ref, br_ref, nx_ref, nu_ref, x_ref, wgu_hbm, bgu_ref, wd_hbm, bd_ref,
                   y_ref, wgu_st, wd_st, wgu_bf, wd_bf, wsem):
    del br_ref
    i = pl.program_id(0)
    d_exp = wd_bf.shape[0]
    used = i < nu_ref[0]
    new_expert = (i == 0) | (be_ref[i] != be_ref[jnp.maximum(i - 1, 0)])

    @pl.when(jnp.logical_not(used))
    def _():
        y_ref[...] = jnp.zeros_like(y_ref)

    def fetch(e):
        return (pltpu.make_async_copy(wgu_hbm.at[e], wgu_st, wsem.at[0]),
                pltpu.make_async_copy(wd_hbm.at[e], wd_st, wsem.at[1]))

    @pl.when(i == 0)
    def _():
        for cp in fetch(be_ref[0]):
            cp.start()

    @pl.when(used & new_expert)
    def _():
        for cp in fetch(be_ref[i]):
            cp.wait()
        _cast_rows(wgu_st, wgu_bf, WEIGHT_CAST_ROWS)
        _cast_rows(wd_st, wd_bf, WEIGHT_CAST_ROWS)

        @pl.when(nx_ref[i] != be_ref[i])
        def _():
            for cp in fetch(nx_ref[i]):
                cp.start()

    @pl.when(used)
    def _():
        x = _load_token_tiles(x_ref, tm)
        gu = _dot(x.astype(BF16), wgu_bf[...]) + bgu_ref[0]
        glu = jnp.minimum(gu[:, :d_exp], SWIGLU_LIMIT)
        lin = jnp.clip(gu[:, d_exp:], -SWIGLU_LIMIT, SWIGLU_LIMIT)
        act = (lin + 1.0) * glu * jax.nn.sigmoid(SWIGLU_ALPHA * glu)
        _store_token_tiles(y_ref, _dot(act.astype(BF16), wd_bf[...]) + bd_ref[0])


def _experts(block_e, block_row, next_e, n_used, xs, n_blk, w_gu, b_gu, w_down, b_down, tm):
    n_exp, d_model, d_gu = w_gu.shape
    d_exp = w_down.shape[1]

    def b_map(i, be, br, nx, nu):
        return (be[i], 0, 0)

    return pl.pallas_call(
        functools.partial(_expert_kernel, tm),
        grid_spec=pltpu.PrefetchScalarGridSpec(
            num_scalar_prefetch=4, grid=(n_blk,),
            in_specs=[pl.BlockSpec((tm * ROW_TILE, LANES),
                                   lambda i, be, br, nx, nu: (br[i], 0)),
                      pl.BlockSpec(memory_space=pl.ANY),
                      pl.BlockSpec((1, 1, d_gu), b_map),
                      pl.BlockSpec(memory_space=pl.ANY),
                      pl.BlockSpec((1, 1, d_model), b_map)],
            out_specs=pl.BlockSpec((tm * ROW_TILE, LANES), lambda i, be, br, nx, nu: (i, 0)),
            scratch_shapes=[pltpu.VMEM((d_model, d_gu), F32),
                            pltpu.VMEM((d_exp, d_model), F32),
                            pltpu.VMEM((d_model, d_gu), BF16),
                            pltpu.VMEM((d_exp, d_model), BF16),
                            pltpu.SemaphoreType.DMA((2,))]),
        out_shape=jax.ShapeDtypeStruct((n_blk * tm * ROW_TILE, LANES), F32),
        compiler_params=pltpu.CompilerParams(
            dimension_semantics=("arbitrary",), vmem_limit_bytes=VMEM_LIMIT),
        name="experts",
    )(block_e, block_row, next_e, n_used, xs, w_gu, b_gu.reshape(n_exp, 1, d_gu), w_down,
      b_down.reshape(n_exp, 1, d_model))


def _combine_kernel(tt, n_tiles, dcur_ref, dnxt_ref, gate_ref, h1_ref, p_ref, ys_ref,
                    gple_ref, wgate_ref, wproj_ref, gfin_ref, y_ref, buf, sem):
    i = pl.program_id(0)
    slot = lax.rem(i, 2)

    def issue(dref, sl):
        def body(t, c):
            for k in range(TOP_K):
                pltpu.make_async_copy(_token_tile(ys_ref, dref[k * tt + t]),
                                      _token_tile(buf.at[sl, k], t), sem.at[sl]).start()
            return c
        lax.fori_loop(0, tt, body, 0, unroll=ISSUE_UNROLL)

    @pl.when(i == 0)
    def _():
        issue(dcur_ref, 0)

    @pl.when(i + 1 < n_tiles)
    def _():
        issue(dnxt_ref, 1 - slot)

    pltpu.make_async_copy(buf.at[slot], buf.at[slot], sem.at[slot]).wait()

    g = gate_ref[...]
    moe = g[:, 0:1] * _load_token_tiles(buf.at[slot, 0], tt)
    for k in range(1, TOP_K):
        moe = moe + g[:, k:k + 1] * _load_token_tiles(buf.at[slot, k], tt)
    h2 = h1_ref[...] + moe
    ple_gate = jax.nn.sigmoid(_dot(_rms(h2, gple_ref[...]).astype(BF16), wgate_ref[...]))
    proj = _dot(p_ref[...].astype(BF16), wproj_ref[...])
    h3 = h2 + proj * ple_gate
    y_ref[...] = _rms(h3, gfin_ref[...])


def _combine(dest_flat, gates_t, h1, p, ys, w, tt):
    tok, d_model = h1.shape
    d_ple = p.shape[-1]
    n_tiles = tok // tt

    def full(a):
        return pl.BlockSpec(a.shape, lambda i: (0,) * a.ndim)

    weights = [w["g_ple"], w["w_ple_gate"], w["w_ple_proj"], w["g_final"]]
    return pl.pallas_call(
        functools.partial(_combine_kernel, tt, n_tiles),
        grid=(n_tiles,),
        in_specs=[pl.BlockSpec((TOP_K * tt,), lambda i: (i,), memory_space=pltpu.SMEM),
                  pl.BlockSpec((TOP_K * tt,), lambda i: (jnp.minimum(i + 1, n_tiles - 1),),
                               memory_space=pltpu.SMEM),
                  pl.BlockSpec((tt, TOP_K), lambda i: (i, 0)),
                  pl.BlockSpec((tt, d_model), lambda i: (i, 0)),
                  pl.BlockSpec((tt, d_ple), lambda i: (i, 0)),
                  pl.BlockSpec(memory_space=pl.ANY)]
                 + [full(a) for a in weights],
        out_specs=pl.BlockSpec((tt, d_model), lambda i: (i, 0)),
        out_shape=jax.ShapeDtypeStruct((tok, d_model), F32),
        scratch_shapes=[pltpu.VMEM((2, TOP_K, tt * ROW_TILE, LANES), F32),
                        pltpu.SemaphoreType.DMA((2,))],
        compiler_params=pltpu.CompilerParams(
            dimension_semantics=("arbitrary",), vmem_limit_bytes=VMEM_LIMIT),
        name="combine",
    )(dest_flat, dest_flat, gates_t, h1, p, ys, *weights)


def _tile_major(a, tt):
    k, tok = a.shape
    return a.reshape(k, tok // tt, tt).transpose(1, 0, 2).reshape(-1)


def _lookup(table, idx):
    n = table.shape[0]
    hit = idx[..., None] == jnp.arange(n, dtype=idx.dtype)
    return jnp.sum(jnp.where(hit, table, 0), axis=-1)


def _pad_state(state, hist):
    bsz, n, ch = state.shape
    return jnp.concatenate([jnp.zeros((bsz, hist - n, ch), state.dtype), state], axis=1)


def _forward(x_prompt, x_sample, p_prompt, p_sample, state_conv, state_pool, lw, g_final,
             mix_rows, expert_rows, combine_rows):
    bp, sp, d_model = x_prompt.shape
    bs, ss, _ = x_sample.shape
    assert d_model == ROW_TILE * LANES, "token-tile layout needs a 1024-wide model"
    n_exp = lw["b_router"].shape[0]
    cch = lw["w_conv"].shape[-1]
    pch = lw["pool_scale"].shape[-1]
    tp, tsamp = bp * sp, bs * ss
    tm = expert_rows
    cap = -(-(tp + tsamp) // tm) * tm

    wr = lw["w_router"]
    wr_hi = wr.astype(BF16)
    wr_lo = (wr - wr_hi.astype(F32)).astype(BF16)
    wr_pad = jnp.concatenate(
        [wr_hi, wr_lo, jnp.zeros((d_model, LANES - 2 * n_exp), BF16)], axis=1)
    mw = dict(g_mix=lw["g_mix"].reshape(1, -1), w_in=lw["w_in"].astype(BF16),
              w_conv=lw["w_conv"], w_pool=lw["w_pool"].astype(BF16),
              pool_scale=lw["pool_scale"].reshape(1, -1), w_out=lw["w_out"].astype(BF16),
              g_ffn=lw["g_ffn"].reshape(1, -1), w_router=wr_pad,
              b_router=lw["b_router"].reshape(-1, 1))
    cw = dict(g_ple=lw["g_ple"].reshape(1, -1), w_ple_gate=lw["w_ple_gate"].astype(BF16),
              w_ple_proj=lw["w_ple_proj"].astype(BF16), g_final=g_final.reshape(1, -1))

    zc = jnp.zeros((bp, CONV_HIST, cch), F32)
    zp = jnp.zeros((bp, POOL_HIST, pch), F32)
    ts_p = min(mix_rows, sp)
    (h1_p, e_p, g_p, r_p, c_p, conv_p, pool_p, xs) = _mixer(
        x_prompt, zc, zp, jnp.zeros((n_exp, LANES), I32), None, n_exp * cap, cap, 0, mw, 1, ts_p)
    (h1_s, e_s, g_s, r_s, c_s, conv_s, pool_s, xs) = _mixer(
        x_sample, _pad_state(state_conv, CONV_HIST), _pad_state(state_pool, POOL_HIST),
        c_p, xs, n_exp * cap, cap, PAST_LEN, mw, bs, ss)

    counts = c_s[:, 0]
    nblk_e = (counts + tm - 1) // tm
    bend = jnp.cumsum(nblk_e)
    bstart = bend - nblk_e
    n_assign = TOP_K * (tp + tsamp)
    n_blk = -(-n_assign // tm) + n_exp
    n_used = bend[-1].astype(I32)
    blk = jnp.minimum(jnp.arange(n_blk, dtype=I32), jnp.maximum(n_used - 1, 0))
    block_e = jnp.minimum(jnp.sum(bend[None, :] <= blk[:, None], axis=1), n_exp - 1).astype(I32)
    block_row = (block_e * (cap // tm) + blk - _lookup(bstart, block_e)).astype(I32)
    after = jnp.minimum(_lookup(bend, block_e), jnp.maximum(n_used - 1, 0))
    next_e = jnp.minimum(jnp.sum(bend[None, :] <= after[:, None], axis=1), n_exp - 1).astype(I32)
    expert_ids = jnp.arange(n_exp, dtype=I32)
    xs = _zero_tails((expert_ids * cap + counts).astype(I32),
                     (nblk_e * tm - counts).astype(I32), xs, tm - 1)

    ys = _experts(block_e, block_row, next_e, n_used.reshape(1), xs, n_blk, lw["w_gu"],
                  lw["b_gu"], lw["w_down"], lw["b_down"], tm)

    dest_p = _lookup(bstart * tm, e_p) + r_p
    dest_s = _lookup(bstart * tm, e_s) + r_s
    tc_p = min(combine_rows, tp)
    y_p = _combine(_tile_major(dest_p, tc_p), g_p.T, h1_p.reshape(tp, d_model),
                   p_prompt.reshape(tp, -1), ys, cw, tc_p)
    y_s = _combine(_tile_major(dest_s, tsamp), g_s.T, h1_s.reshape(tsamp, d_model),
                   p_sample.reshape(tsamp, -1), ys, cw, tsamp)
    return (y_p.reshape(bp, sp, d_model), y_s.reshape(bs, ss, d_model),
            conv_p, pool_p, conv_s, pool_s)


def kernel(x_prompt, x_sample, p_prompt, p_sample, state_conv, state_pool, g_mix, w_in, w_conv,
           w_pool, pool_scale, w_out, g_ffn, w_router, b_router, w_gu, b_gu, w_down, b_down,
           g_ple, w_ple_proj, w_ple_gate, g_final):
    assert g_mix.shape[0] == 1, "single-layer step"
    lw = dict(g_mix=g_mix[0], w_in=w_in[0], w_conv=w_conv[0], w_pool=w_pool[0],
              pool_scale=pool_scale[0], w_out=w_out[0], g_ffn=g_ffn[0], w_router=w_router[0],
              b_router=b_router[0], w_gu=w_gu[0], b_gu=b_gu[0], w_down=w_down[0],
              b_down=b_down[0], g_ple=g_ple[0], w_ple_proj=w_ple_proj[0],
              w_ple_gate=w_ple_gate[0])
    y_p, y_s, conv_p, pool_p, conv_s, pool_s = _forward(
        x_prompt, x_sample, p_prompt[0], p_sample[0], state_conv[0], state_pool[0], lw, g_final,
        MIX_ROWS, EXPERT_ROWS, COMBINE_ROWS)
    return (y_p, y_s, conv_p[None], pool_p[None], conv_s[None], pool_s[None])
```

```python
import functools

import jax
import jax.numpy as jnp
from jax import lax
from jax.experimental import pallas as pl
from jax.experimental.pallas import tpu as pltpu

F32 = jnp.float32
BF16 = jnp.bfloat16
I32 = jnp.int32

PAST_LEN = 4096
POOL_WINDOWS = (2, 4, 8, 16)
TOP_K = 4
SWIGLU_LIMIT = 7.0
SWIGLU_ALPHA = 1.702
RMS_EPS = 1e-6

CONV_HIST = 8
POOL_HIST = 16
LANES = 128
ROW_TILE = 8
VMEM_LIMIT = 56 * 1024 * 1024

MIX_ROWS = 512
EXPERT_ROWS = 512
COMBINE_ROWS = 256
WEIGHT_CAST_ROWS = 128
ISSUE_UNROLL = 8
SCATTER_CHUNKS = 4


def _rms(x, g):
    return x * lax.rsqrt(jnp.mean(x * x, axis=-1, keepdims=True) + RMS_EPS) * g


def _dot(a, b):
    return jnp.dot(a, b, preferred_element_type=F32)


def _store_token_tiles(ref, x):
    n = x.shape[0]
    for j in range(ROW_TILE):
        ref[pl.ds(j, n, stride=ROW_TILE), :] = x[:, j * LANES:(j + 1) * LANES]


def _load_token_tiles(ref, n):
    return jnp.concatenate(
        [ref[pl.ds(j, n, stride=ROW_TILE), :] for j in range(ROW_TILE)], axis=1)


def _token_tile(ref, t):
    return ref.at[pl.ds(pl.multiple_of(t * ROW_TILE, ROW_TILE), ROW_TILE)]


MIXER_INPUTS = 14


def _mixer_kernel(start_pos, nb, ts, n_exp, cap, aliased, *refs):
    if aliased:
        refs = refs[:MIXER_INPUTS] + refs[MIXER_INPUTS + 1:]
    (x_ref, convp_ref, poolp_ref, cnt0_ref, gmix_ref, win_ref, wconv_ref, wpool_ref,
     pscale_ref, wout_ref, gffn_ref, wr_ref, br_ref, tri_ref,
     h1_ref, tope_ref, gate_ref, rank_ref, cnt_ref, convn_ref, pooln_ref, xs_ref,
     ev_ref, eu_ref, carry_ref, xf_st, dvm, dsm, sems, dsem) = refs
    b = pl.program_id(0)
    s = pl.program_id(1)
    step = b * pl.num_programs(1) + s
    n_steps = pl.num_programs(0) * pl.num_programs(1)
    slot = lax.rem(step, 2)
    rows = nb * ts
    d_model = x_ref.shape[-1]
    cch = wconv_ref.shape[-1]
    gch = wpool_ref.shape[-1]
    n_grp = wpool_ref.shape[0]

    @pl.when(step == 0)
    def _():
        carry_ref[...] = cnt0_ref[...].astype(F32)

    @pl.when(s == 0)
    def _():
        ev_ref[:, 0:CONV_HIST, :] = convp_ref[...]
        eu_ref[:, 0:POOL_HIST, :] = poolp_ref[...]

    def scatter(sl, lo, n):
        def issue(t, c):
            src = _token_tile(xf_st.at[sl], t)
            for k in range(TOP_K):
                pltpu.make_async_copy(src, _token_tile(xs_ref, dsm[sl, k, t]),
                                      sems.at[sl]).start()
            return c
        lax.fori_loop(lo, lo + n, issue, 0, unroll=ISSUE_UNROLL)

    def scatter_prev_chunks(c, n):
        @pl.when(step > 0)
        def _():
            scatter(1 - slot, c * (rows // SCATTER_CHUNKS), n * (rows // SCATTER_CHUNKS))

    def wait_tile(sl):
        n = TOP_K * rows * ROW_TILE
        pltpu.make_async_copy(xs_ref.at[pl.ds(0, n)], xs_ref.at[pl.ds(0, n)], sems.at[sl]).wait()

    x = x_ref[...].reshape(rows, d_model)
    xn = _rms(x, gmix_ref[...])
    z = _dot(xn.astype(BF16), win_ref[...])
    scatter_prev_chunks(0, 1)
    bg = z[:, 0:cch]
    v = z[:, cch:2 * cch] * z[:, 2 * cch:3 * cch]
    u = z[:, 3 * cch:]

    w0 = wconv_ref[0:1, :]
    w1 = wconv_ref[1:2, :]
    w2 = wconv_ref[2:3, :]
    pos = start_pos + s * ts + lax.broadcasted_iota(I32, (ts, 1), 0)

    mix_rows = []
    for i in range(nb):
        r0 = i * ts
        v_i = v[r0:r0 + ts]
        u_i = u[r0:r0 + ts]
        ev_ref[i, CONV_HIST:CONV_HIST + ts, :] = v_i
        eu_ref[i, POOL_HIST:POOL_HIST + ts, :] = u_i
        conv = (w0 * ev_ref[i, CONV_HIST - 2:CONV_HIST - 2 + ts, :]
                + w1 * ev_ref[i, CONV_HIST - 1:CONV_HIST - 1 + ts, :]
                + w2 * v_i)
        y_a = bg[r0:r0 + ts] * conv

        e = eu_ref[i]
        sums = []
        cur = e
        shift = 1
        for g in range(n_grp):
            cur = cur + pltpu.roll(cur, shift, axis=0)
            sums.append(cur[POOL_HIST:POOL_HIST + ts, 0:gch])
            if g + 1 < n_grp:
                cur = cur[:, gch:]
            shift *= 2
        yb = []
        for g, w in enumerate(POOL_WINDOWS):
            cnt = jnp.minimum(pos + 1, w).astype(F32)
            diff = sums[g] / cnt - u_i[:, g * gch:(g + 1) * gch]
            yb.append(_dot(diff.astype(BF16), wpool_ref[g]))
        y_b = jnp.concatenate(yb, axis=1) * pscale_ref[...]
        mix_rows.append(jnp.concatenate([y_a, y_b], axis=1))

        convn_ref[i] = ev_ref[i, ts + CONV_HIST - 2:ts + CONV_HIST, :]
        pooln_ref[i] = eu_ref[i, ts + 1:ts + POOL_HIST, :]
        ev_ref[i, 0:CONV_HIST, :] = ev_ref[i, ts:ts + CONV_HIST, :]
        eu_ref[i, 0:POOL_HIST, :] = eu_ref[i, ts:ts + POOL_HIST, :]

    mix = mix_rows[0] if nb == 1 else jnp.concatenate(mix_rows, axis=0)
    h1 = x + _dot(mix.astype(BF16), wout_ref[...])
    h1_ref[...] = h1.reshape(nb, ts, d_model)
    xf = _rms(h1, gffn_ref[...])

    @pl.when(step > 1)
    def _():
        wait_tile(slot)
    _store_token_tiles(xf_st.at[slot], xf)
    scatter_prev_chunks(1, 1)

    lg = _dot(xf.astype(BF16), wr_ref[...])
    lgt = lg.T
    logits = lgt[0:n_exp] + lgt[n_exp:2 * n_exp] + br_ref[...]

    iota_e = lax.broadcasted_iota(I32, logits.shape, 0)
    top_l, top_e = [], []
    sel = jnp.zeros(logits.shape, F32)
    cur = logits
    for _ in range(TOP_K):
        m = jnp.max(cur, axis=0, keepdims=True)
        idx = jnp.min(jnp.where(cur == m, iota_e, n_exp), axis=0, keepdims=True)
        hit = iota_e == idx
        top_l.append(m)
        top_e.append(idx)
        sel = jnp.where(hit, 1.0, sel)
        cur = jnp.where(hit, -jnp.inf, cur)
    ex = [jnp.exp(l - top_l[0]) for l in top_l]
    den = ex[0] + ex[1] + ex[2] + ex[3]
    gate_ref[...] = jnp.concatenate([e_ / den for e_ in ex], axis=0)
    tope = jnp.concatenate(top_e, axis=0)
    tope_ref[...] = tope

    excl = _dot(sel.astype(BF16), tri_ref[...]) + carry_ref[:, 0:1]
    ranks = [jnp.sum(jnp.where(iota_e == e_, excl, 0.0), axis=0, keepdims=True) for e_ in top_e]
    rank = jnp.concatenate(ranks, axis=0).astype(I32)
    rank_ref[...] = rank
    carry = carry_ref[...] + jnp.sum(sel, axis=1, keepdims=True)
    carry_ref[...] = carry
    cnt_ref[...] = carry.astype(I32)

    dvm[...] = tope * cap + rank
    to_smem = pltpu.make_async_copy(dvm, dsm.at[slot], dsem)
    to_smem.start()
    scatter_prev_chunks(2, SCATTER_CHUNKS - 2)
    to_smem.wait()

    @pl.when(step == n_steps - 1)
    def _():
        scatter(slot, 0, rows)

        @pl.when(step > 0)
        def _():
            wait_tile(1 - slot)
        wait_tile(slot)


def _mixer(x, conv_prev, pool_prev, cnt0, xs, xs_rows, cap, start_pos, w, nb, ts):
    bsz, seq, d_model = x.shape
    n_s = seq // ts
    n_b = bsz // nb
    rows = nb * ts
    tok = bsz * seq
    cch = w["w_conv"].shape[-1]
    pch = w["pool_scale"].shape[-1]
    n_exp = w["b_router"].shape[0]
    tri = (lax.broadcasted_iota(I32, (rows, rows), 0)
           < lax.broadcasted_iota(I32, (rows, rows), 1)).astype(BF16)

    def full(a):
        return pl.BlockSpec(a.shape, lambda b, s: (0,) * a.ndim)

    weights = [w["g_mix"], w["w_in"], w["w_conv"], w["w_pool"], w["pool_scale"], w["w_out"],
               w["g_ffn"], w["w_router"], w["b_router"], tri]
    tok_spec = pl.BlockSpec((TOP_K, rows), lambda b, s: (0, b * n_s + s))
    operands = [x, conv_prev, pool_prev, cnt0] + weights
    assert len(operands) == MIXER_INPUTS
    in_specs = ([pl.BlockSpec((nb, ts, d_model), lambda b, s: (b, s, 0)),
                 pl.BlockSpec((nb, CONV_HIST, cch), lambda b, s: (b, 0, 0)),
                 pl.BlockSpec((nb, POOL_HIST, pch), lambda b, s: (b, 0, 0)),
                 full(cnt0)] + [full(a) for a in weights])
    aliases = {}
    if xs is not None:
        operands.append(xs)
        in_specs.append(pl.BlockSpec(memory_space=pl.ANY))
        aliases = {MIXER_INPUTS: 7}
    return pl.pallas_call(
        functools.partial(_mixer_kernel, start_pos, nb, ts, n_exp, cap, xs is not None),
        grid=(n_b, n_s),
        in_specs=in_specs,
        out_specs=[pl.BlockSpec((nb, ts, d_model), lambda b, s: (b, s, 0)),
                   tok_spec, tok_spec, tok_spec,
                   pl.BlockSpec((n_exp, LANES), lambda b, s: (0, 0)),
                   pl.BlockSpec((nb, 2, cch), lambda b, s: (b, 0, 0)),
                   pl.BlockSpec((nb, POOL_HIST - 1, pch), lambda b, s: (b, 0, 0)),
                   pl.BlockSpec(memory_space=pl.ANY)],
        out_shape=[jax.ShapeDtypeStruct((bsz, seq, d_model), F32),
                   jax.ShapeDtypeStruct((TOP_K, tok), I32),
                   jax.ShapeDtypeStruct((TOP_K, tok), F32),
                   jax.ShapeDtypeStruct((TOP_K, tok), I32),
                   jax.ShapeDtypeStruct((n_exp, LANES), I32),
                   jax.ShapeDtypeStruct((bsz, 2, cch), F32),
                   jax.ShapeDtypeStruct((bsz, POOL_HIST - 1, pch), F32),
                   jax.ShapeDtypeStruct((xs_rows * ROW_TILE, LANES), F32)],
        scratch_shapes=[pltpu.VMEM((nb, ts + CONV_HIST, cch), F32),
                        pltpu.VMEM((nb, ts + POOL_HIST, pch), F32),
                        pltpu.VMEM((n_exp, LANES), F32),
                        pltpu.VMEM((2, rows * ROW_TILE, LANES), F32),
                        pltpu.VMEM((TOP_K, rows), I32),
                        pltpu.SMEM((2, TOP_K, rows), I32),
                        pltpu.SemaphoreType.DMA((2,)),
                        pltpu.SemaphoreType.DMA],
        input_output_aliases=aliases,
        compiler_params=pltpu.CompilerParams(
            dimension_semantics=("arbitrary", "arbitrary"), vmem_limit_bytes=VMEM_LIMIT),
        name="mixer",
    )(*operands)


def _tails_kernel(n_exp, n_bits, zs_ref, zn_ref, xs_in_ref, xs_ref, zero_ref, sem):
    del xs_in_ref
    zero_ref[...] = jnp.zeros_like(zero_ref)

    def each_piece(fn):
        def per_expert(e, c):
            zn = zn_ref[e]
            for p in reversed(range(n_bits)):
                size = 1 << p
                done = (zn >> (p + 1)) << (p + 1)

                @pl.when((zn & size) != 0)
                def _():
                    dst = pl.multiple_of((zs_ref[e] + done) * ROW_TILE, ROW_TILE)
                    fn(pltpu.make_async_copy(zero_ref.at[pl.ds(0, size * ROW_TILE)],
                                             xs_ref.at[pl.ds(dst, size * ROW_TILE)], sem))
            return c
        lax.fori_loop(0, n_exp, per_expert, 0)

    each_piece(lambda cp: cp.start())
    each_piece(lambda cp: cp.wait())


def _zero_tails(zstart, zcount, xs, max_count):
    n_bits = max_count.bit_length()
    return pl.pallas_call(
        functools.partial(_tails_kernel, zstart.shape[0], n_bits),
        grid_spec=pltpu.PrefetchScalarGridSpec(
            num_scalar_prefetch=2, grid=(1,),
            in_specs=[pl.BlockSpec(memory_space=pl.ANY)],
            out_specs=pl.BlockSpec(memory_space=pl.ANY),
            scratch_shapes=[pltpu.VMEM(((1 << (n_bits - 1)) * ROW_TILE, LANES), F32),
                            pltpu.SemaphoreType.DMA]),
        out_shape=jax.ShapeDtypeStruct(xs.shape, xs.dtype),
        input_output_aliases={2: 0},
        compiler_params=pltpu.CompilerParams(dimension_semantics=("arbitrary",)),
        name="zero_tails",
    )(zstart, zcount, xs)


def _cast_rows(src_ref, dst_ref, chunk):
    def body(c, carry):
        r = pl.multiple_of(c * chunk, chunk)
        dst_ref[pl.ds(r, chunk), :] = src_ref[pl.ds(r, chunk), :].astype(BF16)
        return carry
    lax.fori_loop(0, src_ref.shape[0] // chunk, body, 0)


def _expert_kernel(tm, be_ref, br_ref, nx_ref, nu_ref, x_ref, wgu_hbm, bgu_ref, wd_hbm, bd_ref,
                   y_ref, wgu_st, wd_st, wgu_bf, wd_bf, wsem):
    del br_ref
    i = pl.program_id(0)
    d_exp = wd_bf.shape[0]
    used = i < nu_ref[0]
    new_expert = (i == 0) | (be_ref[i] != be_ref[jnp.maximum(i - 1, 0)])

    @pl.when(jnp.logical_not(used))
    def _():
        y_ref[...] = jnp.zeros_like(y_ref)

    def fetch(e):
        return (pltpu.make_async_copy(wgu_hbm.at[e], wgu_st, wsem.at[0]),
                pltpu.make_async_copy(wd_hbm.at[e], wd_st, wsem.at[1]))

    @pl.when(i == 0)
    def _():
        for cp in fetch(be_ref[0]):
            cp.start()

    @pl.when(used & new_expert)
    def _():
        for cp in fetch(be_ref[i]):
            cp.wait()
        _cast_rows(wgu_st, wgu_bf, WEIGHT_CAST_ROWS)
        _cast_rows(wd_st, wd_bf, WEIGHT_CAST_ROWS)

        @pl.when(nx_ref[i] != be_ref[i])
        def _():
            for cp in fetch(nx_ref[i]):
                cp.start()

    @pl.when(used)
    def _():
        x = _load_token_tiles(x_ref, tm)
        gu = _dot(x.astype(BF16), wgu_bf[...]) + bgu_ref[0]
        glu = jnp.minimum(gu[:, :d_exp], SWIGLU_LIMIT)
        lin = jnp.clip(gu[:, d_exp:], -SWIGLU_LIMIT, SWIGLU_LIMIT)
        act = (lin + 1.0) * glu * jax.nn.sigmoid(SWIGLU_ALPHA * glu)
        _store_token_tiles(y_ref, _dot(act.astype(BF16), wd_bf[...]) + bd_ref[0])


def _experts(block_e, block_row, next_e, n_used, xs, n_blk, w_gu, b_gu, w_down, b_down, tm):
    n_exp, d_model, d_gu = w_gu.shape
    d_exp = w_down.shape[1]

    def b_map(i, be, br, nx, nu):
        return (be[i], 0, 0)

    return pl.pallas_call(
        functools.partial(_expert_kernel, tm),
        grid_spec=pltpu.PrefetchScalarGridSpec(
            num_scalar_prefetch=4, grid=(n_blk,),
            in_specs=[pl.BlockSpec((tm * ROW_TILE, LANES),
                                   lambda i, be, br, nx, nu: (br[i], 0)),
                      pl.BlockSpec(memory_space=pl.ANY),
                      pl.BlockSpec((1, 1, d_gu), b_map),
                      pl.BlockSpec(memory_space=pl.ANY),
                      pl.BlockSpec((1, 1, d_model), b_map)],
            out_specs=pl.BlockSpec((tm * ROW_TILE, LANES), lambda i, be, br, nx, nu: (i, 0)),
            scratch_shapes=[pltpu.VMEM((d_model, d_gu), F32),
                            pltpu.VMEM((d_exp, d_model), F32),
                            pltpu.VMEM((d_model, d_gu), BF16),
                            pltpu.VMEM((d_exp, d_model), BF16),
                            pltpu.SemaphoreType.DMA((2,))]),
        out_shape=jax.ShapeDtypeStruct((n_blk * tm * ROW_TILE, LANES), F32),
        compiler_params=pltpu.CompilerParams(
            dimension_semantics=("arbitrary",), vmem_limit_bytes=VMEM_LIMIT),
        name="experts",
    )(block_e, block_row, next_e, n_used, xs, w_gu, b_gu.reshape(n_exp, 1, d_gu), w_down,
      b_down.reshape(n_exp, 1, d_model))


def _combine_kernel(tt, n_tiles, dcur_ref, dnxt_ref, gate_ref, h1_ref, p_ref, ys_ref,
                    gple_ref, wgate_ref, wproj_ref, gfin_ref, y_ref, buf, sem):
    i = pl.program_id(0)
    slot = lax.rem(i, 2)

    def issue(dref, sl, lo=0, hi=tt):
        def body(t, c):
            for k in range(TOP_K):
                pltpu.make_async_copy(_token_tile(ys_ref, dref[k * tt + t]),
                                      _token_tile(buf.at[sl, k], t), sem.at[sl]).start()
            return c
        lax.fori_loop(lo, hi, body, 0, unroll=ISSUE_UNROLL)

    @pl.when(i == 0)
    def _():
        issue(dcur_ref, 0)

    @pl.when(i + 1 < n_tiles)
    def _():
        issue(dnxt_ref, 1 - slot, 0, tt // 2)

    pltpu.make_async_copy(buf.at[slot], buf.at[slot], sem.at[slot]).wait()

    g = gate_ref[...]
    moe = g[:, 0:1] * _load_token_tiles(buf.at[slot, 0], tt)
    for k in range(1, TOP_K):
        moe = moe + g[:, k:k + 1] * _load_token_tiles(buf.at[slot, k], tt)
    h2 = h1_ref[...] + moe
    ple_gate = jax.nn.sigmoid(_dot(_rms(h2, gple_ref[...]).astype(BF16), wgate_ref[...]))
    proj = _dot(p_ref[...].astype(BF16), wproj_ref[...])
    h3 = h2 + proj * ple_gate
    y_ref[...] = _rms(h3, gfin_ref[...])

    @pl.when(i + 1 < n_tiles)
    def _():
        issue(dnxt_ref, 1 - slot, tt // 2, tt)


def _combine(dest_flat, gates_t, h1, p, ys, w, tt):
    tok, d_model = h1.shape
    d_ple = p.shape[-1]
    n_tiles = tok // tt

    def full(a):
        return pl.BlockSpec(a.shape, lambda i: (0,) * a.ndim)

    weights = [w["g_ple"], w["w_ple_gate"], w["w_ple_proj"], w["g_final"]]
    return pl.pallas_call(
        functools.partial(_combine_kernel, tt, n_tiles),
        grid=(n_tiles,),
        in_specs=[pl.BlockSpec((TOP_K * tt,), lambda i: (i,), memory_space=pltpu.SMEM),
                  pl.BlockSpec((TOP_K * tt,), lambda i: (jnp.minimum(i + 1, n_tiles - 1),),
                               memory_space=pltpu.SMEM),
                  pl.BlockSpec((tt, TOP_K), lambda i: (i, 0)),
                  pl.BlockSpec((tt, d_model), lambda i: (i, 0)),
                  pl.BlockSpec((tt, d_ple), lambda i: (i, 0)),
                  pl.BlockSpec(memory_space=pl.ANY)]
                 + [full(a) for a in weights],
        out_specs=pl.BlockSpec((tt, d_model), lambda i: (i, 0)),
        out_shape=jax.ShapeDtypeStruct((tok, d_model), F32),
        scratch_shapes=[pltpu.VMEM((2, TOP_K, tt * ROW_TILE, LANES), F32),
                        pltpu.SemaphoreType.DMA((2,))],
        compiler_params=pltpu.CompilerParams(
            dimension_semantics=("arbitrary",), vmem_limit_bytes=VMEM_LIMIT),
        name="combine",
    )(dest_flat, dest_flat, gates_t, h1, p, ys, *weights)


def _tile_major(a, tt):
    k, tok = a.shape
    return a.reshape(k, tok // tt, tt).transpose(1, 0, 2).reshape(-1)


def _lookup(table, idx):
    n = table.shape[0]
    hit = idx[..., None] == jnp.arange(n, dtype=idx.dtype)
    return jnp.sum(jnp.where(hit, table, 0), axis=-1)


def _pad_state(state, hist):
    bsz, n, ch = state.shape
    return jnp.concatenate([jnp.zeros((bsz, hist - n, ch), state.dtype), state], axis=1)


def _forward(x_prompt, x_sample, p_prompt, p_sample, state_conv, state_pool, lw, g_final,
             mix_rows, expert_rows, combine_rows):
    bp, sp, d_model = x_prompt.shape
    bs, ss, _ = x_sample.shape
    assert d_model == ROW_TILE * LANES, "token-tile layout needs a 1024-wide model"
    n_exp = lw["b_router"].shape[0]
    cch = lw["w_conv"].shape[-1]
    pch = lw["pool_scale"].shape[-1]
    tp, tsamp = bp * sp, bs * ss
    tm = expert_rows
    cap = -(-(tp + tsamp) // tm) * tm

    wr = lw["w_router"]
    wr_hi = wr.astype(BF16)
    wr_lo = (wr - wr_hi.astype(F32)).astype(BF16)
    wr_pad = jnp.concatenate(
        [wr_hi, wr_lo, jnp.zeros((d_model, LANES - 2 * n_exp), BF16)], axis=1)
    mw = dict(g_mix=lw["g_mix"].reshape(1, -1), w_in=lw["w_in"].astype(BF16),
              w_conv=lw["w_conv"], w_pool=lw["w_pool"].astype(BF16),
              pool_scale=lw["pool_scale"].reshape(1, -1), w_out=lw["w_out"].astype(BF16),
              g_ffn=lw["g_ffn"].reshape(1, -1), w_router=wr_pad,
              b_router=lw["b_router"].reshape(-1, 1))
    cw = dict(g_ple=lw["g_ple"].reshape(1, -1), w_ple_gate=lw["w_ple_gate"].astype(BF16),
              w_ple_proj=lw["w_ple_proj"].astype(BF16), g_final=g_final.reshape(1, -1))

    zc = jnp.zeros((bp, CONV_HIST, cch), F32)
    zp = jnp.zeros((bp, POOL_HIST, pch), F32)
    ts_p = min(mix_rows, sp)
    (h1_p, e_p, g_p, r_p, c_p, conv_p, pool_p, xs) = _mixer(
        x_prompt, zc, zp, jnp.zeros((n_exp, LANES), I32), None, n_exp * cap, cap, 0, mw, 1, ts_p)
    (h1_s, e_s, g_s, r_s, c_s, conv_s, pool_s, xs) = _mixer(
        x_sample, _pad_state(state_conv, CONV_HIST), _pad_state(state_pool, POOL_HIST),
        c_p, xs, n_exp * cap, cap, PAST_LEN, mw, bs, ss)

    counts = c_s[:, 0]
    nblk_e = (counts + tm - 1) // tm
    bend = jnp.cumsum(nblk_e)
    bstart = bend - nblk_e
    n_assign = TOP_K * (tp + tsamp)
    n_blk = -(-n_assign // tm) + n_exp
    n_used = bend[-1].astype(I32)
    blk = jnp.minimum(jnp.arange(n_blk, dtype=I32), jnp.maximum(n_used - 1, 0))
    block_e = jnp.minimum(jnp.sum(bend[None, :] <= blk[:, None], axis=1), n_exp - 1).astype(I32)
    block_row = (block_e * (cap // tm) + blk - _lookup(bstart, block_e)).astype(I32)
    after = jnp.minimum(_lookup(bend, block_e), jnp.maximum(n_used - 1, 0))
    next_e = jnp.minimum(jnp.sum(bend[None, :] <= after[:, None], axis=1), n_exp - 1).astype(I32)
    expert_ids = jnp.arange(n_exp, dtype=I32)
    xs = _zero_tails((expert_ids * cap + counts).astype(I32),
                     (nblk_e * tm - counts).astype(I32), xs, tm - 1)

    ys = _experts(block_e, block_row, next_e, n_used.reshape(1), xs, n_blk, lw["w_gu"],
                  lw["b_gu"], lw["w_down"], lw["b_down"], tm)

    dest_p = _lookup(bstart * tm, e_p) + r_p
    dest_s = _lookup(bstart * tm, e_s) + r_s
    tc_p = min(combine_rows, tp)
    y_p = _combine(_tile_major(dest_p, tc_p), g_p.T, h1_p.reshape(tp, d_model),
                   p_prompt.reshape(tp, -1), ys, cw, tc_p)
    y_s = _combine(_tile_major(dest_s, tsamp), g_s.T, h1_s.reshape(tsamp, d_model),
                   p_sample.reshape(tsamp, -1), ys, cw, tsamp)
    return (y_p.reshape(bp, sp, d_model), y_s.reshape(bs, ss, d_model),
            conv_p, pool_p, conv_s, pool_s)


def kernel(x_prompt, x_sample, p_prompt, p_sample, state_conv, state_pool, g_mix, w_in, w_conv,
           w_pool, pool_scale, w_out, g_ffn, w_router, b_router, w_gu, b_gu, w_down, b_down,
           g_ple, w_ple_proj, w_ple_gate, g_final):
    assert g_mix.shape[0] == 1, "single-layer step"
    lw = dict(g_mix=g_mix[0], w_in=w_in[0], w_conv=w_conv[0], w_pool=w_pool[0],
              pool_scale=pool_scale[0], w_out=w_out[0], g_ffn=g_ffn[0], w_router=w_router[0],
              b_router=b_router[0], w_gu=w_gu[0], b_gu=b_gu[0], w_down=w_down[0],
              b_down=b_down[0], g_ple=g_ple[0], w_ple_proj=w_ple_proj[0],
              w_ple_gate=w_ple_gate[0])
    y_p, y_s, conv_p, pool_p, conv_s, pool_s = _forward(
        x_prompt, x_sample, p_prompt[0], p_sample[0], state_conv[0], state_pool[0], lw, g_final,
        MIX_ROWS, EXPERT_ROWS, COMBINE_ROWS)
    return (y_p, y_s, conv_p[None], pool_p[None], conv_s[None], pool_s[None])
```
